```python
import jax, jax.numpy as jnp
from jax import lax
import numpy as np

D_MODEL = 2048
BATCH = 16
SEQ = 2048
DEPTH = 1
DEC_BATCH = 32
DEC_SEQ = 16
PAST_LEN = 1024

CHUNK = 64
D_MIX = D_MODEL
N_HEADS = 8
HEAD_DIM = 128
D_ATTN = N_HEADS * HEAD_DIM
D_CONV = D_MIX - D_ATTN
CONV_WIDTH = 31
IDX_HEADS = 16
IDX_DIM = 64
IDX_TOPK_MAX = 256
QBLK = 64
N_EXPERTS = 256
TOP_K = 8
N_GROUPS = 8
TOPK_GROUPS = 4
D_EXPERT = 512
D_SHARED = 512
ROUTED_SCALE = 2.5
MOE_BLOCK = 128
EPS = 1e-6
NEG_INF = -1e30
IN_SIZES = (D_ATTN, D_ATTN, D_ATTN, IDX_HEADS * IDX_DIM, IDX_DIM, IDX_HEADS, D_CONV, D_CONV)
D_IN = 3 * D_ATTN + IDX_HEADS * IDX_DIM + IDX_DIM + IDX_HEADS + 2 * D_CONV

kernel_name = "hybrid_stream_dsa_conformer_moe_step"


def rmsnorm(x, g):
    xf = x.astype(jnp.float32)
    y = xf * lax.rsqrt(jnp.mean(xf * xf, axis=-1, keepdims=True) + EPS)
    return (y * g.astype(jnp.float32)).astype(x.dtype)


def layernorm(x, g, b):
    xf = x.astype(jnp.float32)
    mu = jnp.mean(xf, axis=-1, keepdims=True)
    xc = xf - mu
    var = jnp.mean(xc * xc, axis=-1, keepdims=True)
    return (xc * lax.rsqrt(var + EPS) * g.astype(jnp.float32) + b.astype(jnp.float32)).astype(x.dtype)


def alibi_slopes():
    return jnp.power(2.0, -8.0 * jnp.arange(1, N_HEADS + 1, dtype=jnp.float32) / N_HEADS)


def ada_mod(c, w_ada, b_ada):
    m = jax.nn.silu(c) @ w_ada + b_ada
    return jnp.split(m[:, None, :], 6, axis=-1)


def modulate(x, g, shift, scale):
    return rmsnorm(x, g) * (1 + scale) + shift


def project_mixer_inputs(h, w_in, b_in, q_norm_g, k_norm_g, idx_k_ln_g, idx_k_ln_b):
    B, T = h.shape[0], h.shape[1]
    z = h @ w_in + b_in
    pts = np.cumsum(IN_SIZES)[:-1].tolist()
    q, k, v, qi, ki, wi, u_val, u_gate = jnp.split(z, pts, axis=-1)
    q = rmsnorm(q.reshape(B, T, N_HEADS, HEAD_DIM), q_norm_g)
    k = rmsnorm(k.reshape(B, T, N_HEADS, HEAD_DIM), k_norm_g)
    v = v.reshape(B, T, N_HEADS, HEAD_DIM)
    qi = qi.reshape(B, T, IDX_HEADS, IDX_DIM) * (IDX_DIM ** -0.5)
    ki = layernorm(ki, idx_k_ln_g, idx_k_ln_b)
    wi = wi * (IDX_HEADS ** -0.5)
    glu = u_val * jax.nn.sigmoid(u_gate)
    return q, k, v, qi, ki, wi, glu


def dsa_block(q, k, v, qi, ki, wi, q_pos, k_pos, topk):
    visible = (k_pos[None, :] // CHUNK) <= (q_pos[:, None] // CHUNK)
    rel = jax.nn.relu(jnp.einsum('qhd,sd->qhs', qi, ki))
    score = jnp.einsum('qh,qhs->qs', wi, rel).astype(jnp.float32)
    score = jnp.where(visible, score, -jnp.inf)
    _, sel = lax.top_k(score, topk)
    k_sel = k[sel]
    v_sel = v[sel]
    ok = jnp.take_along_axis(visible, sel, axis=1)
    dist = jnp.abs(q_pos[:, None] - k_pos[sel]).astype(jnp.float32)
    logits = jnp.einsum('qhd,qkhd->qhk', q, k_sel).astype(jnp.float32) * (HEAD_DIM ** -0.5)
    logits = logits - alibi_slopes()[None, :, None] * dist[:, None, :]
    logits = jnp.where(ok[:, None, :], logits, NEG_INF)
    p = jax.nn.softmax(logits, axis=-1).astype(v.dtype)
    return jnp.einsum('qhk,qkhd->qhd', p, v_sel)


def dsa_prompt(q, k, v, qi, ki, wi, topk):
    B, T = q.shape[0], q.shape[1]
    nblk = T // QBLK
    k_pos = jnp.arange(T, dtype=jnp.int32)
    flat = jnp.arange(B * nblk, dtype=jnp.int32)
    b_ids, j_ids = flat // nblk, flat % nblk

    def one(ids):
        b, j = ids
        s0 = j * QBLK
        blk = lambda a: lax.dynamic_slice_in_dim(a[b], s0, QBLK, axis=0)
        q_pos = s0 + jnp.arange(QBLK, dtype=jnp.int32)
        return dsa_block(blk(q), k[b], v[b], blk(qi), ki[b], blk(wi), q_pos, k_pos, topk)

    o = lax.map(one, (b_ids, j_ids))
    return o.reshape(B, T, D_ATTN)


def dsa_sample(q, k_all, v_all, qi, ki_all, wi, topk):
    Bd, T, L = q.shape[0], q.shape[1], k_all.shape[1]
    q_pos = (L - T) + jnp.arange(T, dtype=jnp.int32)
    k_pos = jnp.arange(L, dtype=jnp.int32)
    o = lax.map(lambda a: dsa_block(*a, q_pos, k_pos, topk), (q, k_all, v_all, qi, ki_all, wi))
    return o.reshape(Bd, T, D_ATTN)


def mixer_output(attn, ext, dw_w, dw_b, conv_ln_g, conv_ln_b, w_out, b_out):
    y = lax.conv_general_dilated(ext, dw_w[:, None, :], window_strides=(1,), padding='VALID',
                                 dimension_numbers=('NWC', 'WIO', 'NWC'),
                                 feature_group_count=ext.shape[-1]) + dw_b
    conv = jax.nn.silu(layernorm(y, conv_ln_g, conv_ln_b))
    return jnp.concatenate([attn, conv], axis=-1) @ w_out + b_out


def route(h2, w_router, router_bias):
    M = h2.shape[0]
    scores = jax.nn.sigmoid((h2 @ w_router).astype(jnp.float32))
    biased = scores + router_bias.astype(jnp.float32)
    grp_score = lax.top_k(biased.reshape(M, N_GROUPS, N_EXPERTS // N_GROUPS), 2)[0].sum(-1)
    _, top_grp = lax.top_k(grp_score, TOPK_GROUPS)
    grp_mask = jnp.any(top_grp[:, :, None] == jnp.arange(N_GROUPS)[None, None, :], axis=1)
    masked = jnp.where(jnp.repeat(grp_mask, N_EXPERTS // N_GROUPS, axis=-1), biased, -jnp.inf)
    _, idx = lax.top_k(masked, TOP_K)
    w = jnp.take_along_axis(scores, idx, axis=-1)
    w = w / jnp.sum(w, axis=-1, keepdims=True) * ROUTED_SCALE
    return idx, w


def routed_experts(h2, idx, gw, w_gate_e, w_up_e, w_down_e):
    M, D = h2.shape
    A = M * TOP_K
    flat_e = idx.reshape(A)
    order = jnp.argsort(flat_e).astype(jnp.int32)
    sorted_e = flat_e[order]
    counts = jnp.bincount(flat_e, length=N_EXPERTS)
    padded = (counts + MOE_BLOCK - 1) // MOE_BLOCK * MOE_BLOCK
    pad_end = jnp.cumsum(padded)
    pad_start = pad_end - padded
    start = jnp.cumsum(counts) - counts
    dest = pad_start[sorted_e] + jnp.arange(A, dtype=jnp.int32) - start[sorted_e]
    n_blocks = A // MOE_BLOCK + N_EXPERTS
    P = n_blocks * MOE_BLOCK
    row_tok = jnp.full((P,), M, jnp.int32).at[dest].set(order // TOP_K)
    row_w = jnp.zeros((P,), h2.dtype).at[dest].set(gw.reshape(A)[order])
    blk_e = jnp.minimum(jnp.searchsorted(pad_end, jnp.arange(n_blocks) * MOE_BLOCK, side='right'),
                        N_EXPERTS - 1)
    h_pad = jnp.concatenate([h2, jnp.zeros((1, D), h2.dtype)], axis=0)

    def body(acc, xs):
        tok, wrow, e = xs
        xb = h_pad[tok]
        yb = (jax.nn.silu(xb @ w_gate_e[e]) * (xb @ w_up_e[e])) @ w_down_e[e]
        return acc.at[tok].add(yb * wrow[:, None]), None

    acc, _ = lax.scan(body, jnp.zeros((M + 1, D), h2.dtype),
                      (row_tok.reshape(n_blocks, MOE_BLOCK), row_w.reshape(n_blocks, MOE_BLOCK), blk_e))
    return acc[:M]


def channel_mixer(h, w_router, router_bias, w_gate_e, w_up_e, w_down_e, w_sh_gate, w_sh_up, w_sh_down):
    B, T, D = h.shape
    h2 = h.reshape(B * T, D)
    idx, gw = route(h2, w_router, router_bias)
    routed = routed_experts(h2, idx, gw.astype(h2.dtype), w_gate_e, w_up_e, w_down_e)
    shared = (jax.nn.silu(h2 @ w_sh_gate) * (h2 @ w_sh_up)) @ w_sh_down
    return (routed + shared).reshape(B, T, D)


def setup_inputs(seed: int = 0) -> dict:
    key = jax.random.key(seed)
    ks = jax.random.split(key, 32)
    f32 = jnp.float32
    nrm = lambda i, shape, s: s * jax.random.normal(ks[i], shape, f32)
    D, L = D_MODEL, DEPTH
    return {
        "x_prompt": nrm(0, (BATCH, SEQ, D), 1.0),
        "x_sample": nrm(1, (DEC_BATCH, DEC_SEQ, D), 1.0),
        "cache_k": nrm(2, (L, DEC_BATCH, PAST_LEN, N_HEADS, HEAD_DIM), 1.0),
        "cache_v": nrm(3, (L, DEC_BATCH, PAST_LEN, N_HEADS, HEAD_DIM), 1.0),
        "cache_kidx": nrm(4, (L, DEC_BATCH, PAST_LEN, IDX_DIM), 1.0),
        "state_conv": nrm(5, (L, DEC_BATCH, CONV_WIDTH - 1, D_CONV), 0.5),
        "c_prompt": nrm(6, (BATCH, D), 1.0),
        "c_sample": nrm(7, (DEC_BATCH, D), 1.0),
        "norm1_g": 1.0 + nrm(8, (L, D), 0.02),
        "norm2_g": 1.0 + nrm(9, (L, D), 0.02),
        "w_ada": nrm(10, (L, D, 6 * D), 0.5 * D ** -0.5),
        "b_ada": nrm(11, (L, 6 * D), 0.01),
        "w_in": nrm(12, (L, D, D_IN), D ** -0.5),
        "b_in": nrm(13, (L, D_IN), 0.01),
        "q_norm_g": 1.0 + nrm(14, (L, HEAD_DIM), 0.02),
        "k_norm_g": 1.0 + nrm(15, (L, HEAD_DIM), 0.02),
        "idx_k_ln_g": 1.0 + nrm(16, (L, IDX_DIM), 0.02),
        "idx_k_ln_b": nrm(17, (L, IDX_DIM), 0.01),
        "dw_w": nrm(18, (L, CONV_WIDTH, D_CONV), CONV_WIDTH ** -0.5),
        "dw_b": nrm(19, (L, D_CONV), 0.01),
        "conv_ln_g": 1.0 + nrm(20, (L, D_CONV), 0.02),
        "conv_ln_b": nrm(21, (L, D_CONV), 0.01),
        "w_out": nrm(22, (L, D_MIX, D), D_MIX ** -0.5),
        "b_out": nrm(23, (L, D), 0.01),
        "w_router": nrm(24, (L, D, N_EXPERTS), D ** -0.5),
        "router_bias": nrm(25, (L, N_EXPERTS), 0.01),
        "w_gate_e": nrm(26, (L, N_EXPERTS, D, D_EXPERT), D ** -0.5),
        "w_up_e": nrm(27, (L, N_EXPERTS, D, D_EXPERT), D ** -0.5),
        "w_down_e": nrm(28, (L, N_EXPERTS, D_EXPERT, D), D_EXPERT ** -0.5),
        "w_sh_gate": nrm(29, (L, D, D_SHARED), D ** -0.5),
        "w_sh_up": nrm(30, (L, D, D_SHARED), D ** -0.5),
        "w_sh_down": nrm(31, (L, D_SHARED, D), D_SHARED ** -0.5),
    }


def reference(x_prompt, x_sample, cache_k, cache_v, cache_kidx, state_conv, c_prompt, c_sample,
              norm1_g, norm2_g, w_ada, b_ada, w_in, b_in, q_norm_g, k_norm_g, idx_k_ln_g, idx_k_ln_b,
              dw_w, dw_b, conv_ln_g, conv_ln_b, w_out, b_out, w_router, router_bias,
              w_gate_e, w_up_e, w_down_e, w_sh_gate, w_sh_up, w_sh_down):
    topk_p = min(IDX_TOPK_MAX, x_prompt.shape[1] // 4)
    topk_s = min(IDX_TOPK_MAX, (cache_k.shape[2] + x_sample.shape[1]) // 4)
    ctx = CONV_WIDTH - 1
    xp, xs = x_prompt, x_sample
    kp, vp, ip, cp = [], [], [], []
    ksm, vsm, ism, csm = [], [], [], []
    for l in range(DEPTH):
        proj = (w_in[l], b_in[l], q_norm_g[l], k_norm_g[l], idx_k_ln_g[l], idx_k_ln_b[l])
        outp = (dw_w[l], dw_b[l], conv_ln_g[l], conv_ln_b[l], w_out[l], b_out[l])
        ffn = (w_router[l], router_bias[l], w_gate_e[l], w_up_e[l], w_down_e[l],
               w_sh_gate[l], w_sh_up[l], w_sh_down[l])
        sh1, sc1, g1, sh2, sc2, g2 = ada_mod(c_prompt, w_ada[l], b_ada[l])
        q, k, v, qi, ki, wi, glu = project_mixer_inputs(modulate(xp, norm1_g[l], sh1, sc1), *proj)
        attn = dsa_prompt(q, k, v, qi, ki, wi, topk_p)
        ext = jnp.pad(glu, ((0, 0), (ctx, 0), (0, 0)))
        xp = xp + g1 * mixer_output(attn, ext, *outp)
        xp = xp + g2 * channel_mixer(modulate(xp, norm2_g[l], sh2, sc2), *ffn)
        kp.append(k)
        vp.append(v)
        ip.append(ki)
        cp.append(glu[:, -ctx:])
        sh1, sc1, g1, sh2, sc2, g2 = ada_mod(c_sample, w_ada[l], b_ada[l])
        q, k, v, qi, ki, wi, glu = project_mixer_inputs(modulate(xs, norm1_g[l], sh1, sc1), *proj)
        attn = dsa_sample(q, jnp.concatenate([cache_k[l], k], axis=1),
                          jnp.concatenate([cache_v[l], v], axis=1), qi,
                          jnp.concatenate([cache_kidx[l], ki], axis=1), wi, topk_s)
        ext = jnp.concatenate([state_conv[l], glu], axis=1)
        xs = xs + g1 * mixer_output(attn, ext, *outp)
        xs = xs + g2 * channel_mixer(modulate(xs, norm2_g[l], sh2, sc2), *ffn)
        ksm.append(k)
        vsm.append(v)
        ism.append(ki)
        csm.append(ext[:, -ctx:])
    return (xp, xs, jnp.stack(kp), jnp.stack(vp), jnp.stack(ip), jnp.stack(cp),
            jnp.stack(ksm), jnp.stack(vsm), jnp.stack(ism), jnp.stack(csm))
```

```python
import functools

import numpy as np
import jax
import jax.numpy as jnp
from jax import lax
from jax.experimental import pallas as pl
from jax.experimental.pallas import tpu as pltpu

F32, BF16, I32 = jnp.float32, jnp.bfloat16, jnp.int32

N_HEADS = 8
HEAD_DIM = 128
D_ATTN = N_HEADS * HEAD_DIM
IDX_HEADS = 16
IDX_DIM = 64
CHUNK = 64
CHUNK_SHIFT = 6
CONV_WIDTH = 31
CTX = CONV_WIDTH - 1
IDX_TOPK_MAX = 256
N_GROUPS = 8
TOPK_GROUPS = 4
TOP_K = 8
ROUTED_SCALE = 2.5
EPS = 1e-6
NEG_INF = -1e30

LANES = 128
SUBLANES = 8
VMEM_LIMIT = 56 * 1024 * 1024

QB = 128
KC = 256
CTX_PAD = 32
MOE_ROWS = 256


def _cparams(sem):
    return pltpu.CompilerParams(dimension_semantics=sem, vmem_limit_bytes=VMEM_LIMIT)


def _const_spec(shape):
    nd = len(shape)
    return pl.BlockSpec(shape, lambda *_: (0,) * nd, pipeline_mode=pl.Buffered(1))


def _silu(x):
    return x * jax.nn.sigmoid(x)


def _ada_kernel(c_ref, w_ref, b_ref, o_ref):
    a = _silu(c_ref[...]).astype(BF16)
    o_ref[...] = jnp.dot(a, w_ref[...].astype(BF16), preferred_element_type=F32) + b_ref[...]


def _ada(c_all, w_ada, b_ada):
    R, D = c_all.shape
    N = w_ada.shape[1]
    TN = 1024
    return pl.pallas_call(
        _ada_kernel,
        grid=(N // TN,),
        in_specs=[pl.BlockSpec((R, D), lambda n: (0, 0)),
                  pl.BlockSpec((D, TN), lambda n: (0, n)),
                  pl.BlockSpec((1, TN), lambda n: (0, n))],
        out_specs=pl.BlockSpec((R, TN), lambda n: (0, n)),
        out_shape=jax.ShapeDtypeStruct((R, N), F32),
        compiler_params=_cparams(("arbitrary",)),
        name="ada",
    )(c_all, w_ada, b_ada.reshape(1, N))


C_Q, C_K, C_V, C_QI = 0, 1024, 2048, 3072
C_KI, C_WI, C_UV, C_UG, C_END = 4096, 4224, 4352, 5376, 6400


def _inproj_kernel(x_ref, sh_ref, sc_ref, g_ref, w_ref, b_ref, qg_ref, kg_ref, lg_ref, lb_ref,
                   q_ref, k_ref, kb_ref, v_ref, vb_ref, qi_ref, ki_ref, kib_ref, wi_ref, glu_ref):
    x = x_ref[...]
    y = x * lax.rsqrt(jnp.mean(x * x, axis=-1, keepdims=True) + EPS) * g_ref[...]
    hb = (y * (1.0 + sc_ref[...]) + sh_ref[...]).astype(BF16)

    def proj(lo, hi):
        return jnp.dot(hb, w_ref[:, lo:hi], preferred_element_type=F32) + b_ref[:, lo:hi]

    def head_norm(z, gain):
        return z * lax.rsqrt(jnp.mean(z * z, axis=-1, keepdims=True) + EPS) * gain

    zq = proj(C_Q, C_K)
    zk = proj(C_K, C_V)
    for h in range(N_HEADS):
        sl = slice(h * HEAD_DIM, (h + 1) * HEAD_DIM)
        q_ref[:, sl] = head_norm(zq[:, sl], qg_ref[...]).astype(BF16)
        kn = head_norm(zk[:, sl], kg_ref[...])
        k_ref[:, sl] = kn
        kb_ref[:, sl] = kn.astype(BF16)
    zv = proj(C_V, C_QI)
    v_ref[...] = zv
    vb_ref[...] = zv.astype(BF16)
    qi_ref[...] = (proj(C_QI, C_KI) * (IDX_DIM ** -0.5)).astype(BF16)
    zki = proj(C_KI, C_WI)[:, :IDX_DIM]
    mu = jnp.mean(zki, axis=-1, keepdims=True)
    xc = zki - mu
    var = jnp.mean(xc * xc, axis=-1, keepdims=True)
    ki = xc * lax.rsqrt(var + EPS) * lg_ref[...] + lb_ref[...]
    ki_ref[...] = ki
    kib_ref[...] = ki.astype(BF16)
    wi_ref[...] = proj(C_WI, C_UV)[:, :IDX_HEADS] * (IDX_HEADS ** -0.5)
    glu_ref[...] = proj(C_UV, C_UG) * jax.nn.sigmoid(proj(C_UG, C_END))


def _mod_spec(per_token, tm, d):
    if per_token:
        return pl.BlockSpec((None, tm, d), lambda b, t: (b, t, 0))
    return pl.BlockSpec((None, 1, d), lambda b, t: (b, 0, 0))


def _inproj(x, sh, sc, g, w2, b2, qg, kg, lg, lb, *, tm, per_token):
    Bx, T, D = x.shape
    tok = lambda n: pl.BlockSpec((None, tm, n), lambda b, t: (b, t, 0))
    shp = lambda n, dt: jax.ShapeDtypeStruct((Bx, T, n), dt)
    return pl.pallas_call(
        _inproj_kernel,
        grid=(Bx, T // tm),
        in_specs=[tok(D), _mod_spec(per_token, tm, D), _mod_spec(per_token, tm, D),
                  _const_spec((1, D)), _const_spec((D, C_END)), _const_spec((1, C_END)),
                  _const_spec((1, HEAD_DIM)), _const_spec((1, HEAD_DIM)),
                  _const_spec((1, IDX_DIM)), _const_spec((1, IDX_DIM))],
        out_specs=[tok(D_ATTN), tok(D_ATTN), tok(D_ATTN), tok(D_ATTN), tok(D_ATTN),
                   tok(IDX_HEADS * IDX_DIM), tok(IDX_DIM), tok(IDX_DIM), tok(IDX_HEADS),
                   tok(D - D_ATTN)],
        out_shape=[shp(D_ATTN, BF16), shp(D_ATTN, F32), shp(D_ATTN, BF16), shp(D_ATTN, F32),
                   shp(D_ATTN, BF16), shp(IDX_HEADS * IDX_DIM, BF16), shp(IDX_DIM, F32),
                   shp(IDX_DIM, BF16), shp(IDX_HEADS, F32), shp(D - D_ATTN, F32)],
        compiler_params=_cparams(("parallel", "arbitrary")),
        name="inproj",
    )(x, sh, sc, g, w2, b2, qg, kg, lg, lb)


def _alibi_slopes():
    return [float(np.float32(2.0) ** np.float32(-8.0 * h / N_HEADS)) for h in range(1, N_HEADS + 1)]


def _dsa_kernel(qT_ref, qiw_ref, wiw_ref, k_ref, vT_ref, ki_ref, o_ref,
                key_ref, sel_ref, acc_ref, m_ref, l_ref, *, q_base, l_valid, topk, causal, n_kc):
    jb = pl.program_id(1)
    nkc = (jb * QB + QB + KC - 1) // KC if causal else n_kc
    lane = lax.broadcasted_iota(I32, (1, QB), 1)
    qpos = q_base + jb * QB + lane
    qchunk = qpos >> CHUNK_SHIFT
    sub = lax.broadcasted_iota(I32, (KC, 1), 0)

    def visible(c):
        kpos = c * KC + sub
        return ((kpos >> CHUNK_SHIFT) <= qchunk) & (kpos < l_valid)

    def score_chunk(c, carry):
        kic = ki_ref[c]
        acc = jnp.zeros((KC, QB), F32)
        for hp in range(IDX_HEADS // 2):
            sl = slice(hp * 2 * QB, (hp + 1) * 2 * QB)
            r = jnp.dot(kic, qiw_ref[:, sl], preferred_element_type=F32)
            t = jnp.maximum(r, 0.0) * wiw_ref[:, sl]
            acc = acc + t[:, :QB] + t[:, QB:]
        bits = pltpu.bitcast(jnp.where(visible(c), acc, -jnp.inf), I32)
        key_ref[c] = bits ^ ((bits >> 31) & 0x7FFFFFFF)
        return carry

    lax.fori_loop(0, nkc, score_chunk, 0)

    def count(pred):
        def body(c, cnt):
            return cnt + jnp.sum(jnp.where(pred(key_ref[c]), 1.0, 0.0), axis=0, keepdims=True)
        return lax.fori_loop(0, nkc, body, jnp.zeros((1, QB), F32))

    kf = float(topk)
    int_min = jnp.full((1, QB), -2 ** 31, I32)
    zero = jnp.zeros((1, QB), I32)
    thr0 = jnp.where(count(lambda key: key >= zero) >= kf, zero, int_min)

    def bisect(i, lo):
        cand = lo | jnp.left_shift(jnp.int32(1), 30 - i)
        return jnp.where(count(lambda key: key >= cand) >= kf, cand, lo)

    thr = lax.fori_loop(0, 31, bisect, thr0)
    need = kf - count(lambda key: key > thr)

    tri = (lax.broadcasted_iota(I32, (KC, KC), 0) >= lax.broadcasted_iota(I32, (KC, KC), 1))
    tri = jnp.where(tri, 1.0, 0.0).astype(BF16)

    def select_chunk(c, carry):
        key = key_ref[c]
        tie = key == thr
        tie_f = jnp.where(tie, 1.0, 0.0)
        pref = jnp.dot(tri, tie_f.astype(BF16), preferred_element_type=F32) + carry
        sel = ((key > thr) | (tie & (pref <= need))) & visible(c)
        sel_ref[c] = jnp.where(sel, 1.0, 0.0)
        return carry + jnp.sum(tie_f, axis=0, keepdims=True)

    lax.fori_loop(0, nkc, select_chunk, jnp.zeros((1, QB), F32))

    m_ref[...] = jnp.full((N_HEADS, QB), NEG_INF, F32)
    l_ref[...] = jnp.zeros((N_HEADS, QB), F32)
    acc_ref[...] = jnp.zeros((D_ATTN, QB), F32)
    slopes = _alibi_slopes()
    scale = HEAD_DIM ** -0.5

    def attend_chunk(c, carry):
        dist = jnp.abs(qpos - (c * KC + sub)).astype(F32)
        selc = sel_ref[c] > 0.5
        for h in range(N_HEADS):
            hs = slice(h * HEAD_DIM, (h + 1) * HEAD_DIM)
            lt = jnp.dot(k_ref[c, :, hs], qT_ref[hs, :], preferred_element_type=F32)
            lt = jnp.where(selc, lt * scale - slopes[h] * dist, NEG_INF)
            m_old = m_ref[h:h + 1, :]
            m_new = jnp.maximum(m_old, jnp.max(lt, axis=0, keepdims=True))
            alpha = jnp.exp(m_old - m_new)
            p = jnp.where(selc, jnp.exp(lt - m_new), 0.0)
            l_ref[h:h + 1, :] = alpha * l_ref[h:h + 1, :] + jnp.sum(p, axis=0, keepdims=True)
            pv = jnp.dot(vT_ref[c, hs, :], p.astype(BF16), preferred_element_type=F32)
            acc_ref[hs, :] = acc_ref[hs, :] * alpha + pv
            m_ref[h:h + 1, :] = m_new
        return carry

    lax.fori_loop(0, nkc, attend_chunk, 0)

    for h in range(N_HEADS):
        hs = slice(h * HEAD_DIM, (h + 1) * HEAD_DIM)
        o = acc_ref[hs, :] / l_ref[h:h + 1, :]
        o_ref[:, hs] = o.T.astype(BF16)


def _dsa(q, qi, wi, k, v, ki, *, q_base, l_valid, topk, causal):
    B, Tq, _ = q.shape
    S = k.shape[1]
    nj, n_kc = Tq // QB, S // KC
    qT = q.transpose(0, 2, 1)
    qiw = qi.reshape(B, nj, QB, IDX_HEADS, IDX_DIM).transpose(0, 1, 4, 3, 2)
    qiw = qiw.reshape(B, nj, IDX_DIM, IDX_HEADS * QB)
    wiw = wi.reshape(B, nj, QB, IDX_HEADS).transpose(0, 1, 3, 2).reshape(B, nj, 1, IDX_HEADS * QB)
    kc = k.reshape(B, n_kc, KC, D_ATTN)
    vT = v.reshape(B, n_kc, KC, D_ATTN).transpose(0, 1, 3, 2)
    kic = ki.reshape(B, n_kc, KC, IDX_DIM)
    kern = functools.partial(_dsa_kernel, q_base=q_base, l_valid=l_valid, topk=topk,
                             causal=causal, n_kc=n_kc)
    return pl.pallas_call(
        kern,
        grid=(B, nj),
        in_specs=[pl.BlockSpec((None, D_ATTN, QB), lambda b, j: (b, 0, j)),
                  pl.BlockSpec((None, None, IDX_DIM, IDX_HEADS * QB), lambda b, j: (b, j, 0, 0)),
                  pl.BlockSpec((None, None, 1, IDX_HEADS * QB), lambda b, j: (b, j, 0, 0)),
                  pl.BlockSpec((None, n_kc, KC, D_ATTN), lambda b, j: (b, 0, 0, 0)),
                  pl.BlockSpec((None, n_kc, D_ATTN, KC), lambda b, j: (b, 0, 0, 0)),
                  pl.BlockSpec((None, n_kc, KC, IDX_DIM), lambda b, j: (b, 0, 0, 0))],
        out_specs=pl.BlockSpec((None, QB, D_ATTN), lambda b, j: (b, j, 0)),
        out_shape=jax.ShapeDtypeStruct((B, Tq, D_ATTN), BF16),
        scratch_shapes=[pltpu.VMEM((n_kc, KC, QB), I32), pltpu.VMEM((n_kc, KC, QB), F32),
                        pltpu.VMEM((D_ATTN, QB), F32), pltpu.VMEM((N_HEADS, QB), F32),
                        pltpu.VMEM((N_HEADS, QB), F32)],
        compiler_params=_cparams(("parallel", "arbitrary")),
        name="dsa",
    )(qT, qiw, wiw, kc, vT, kic)


def _conv_kernel(glu_ref, ctx_ref, w_ref, b_ref, g_ref, be_ref, o_ref, ext_ref, *, T, rt):
    C = glu_ref.shape[-1]
    ext_ref[0:CTX_PAD, :] = ctx_ref[...]
    ext_ref[CTX_PAD:CTX_PAD + T, :] = glu_ref[...]
    ext_ref[CTX_PAD + T:, :] = jnp.zeros((SUBLANES, C), F32)
    first = CTX_PAD - CTX

    def tile(i, carry):
        t0 = pl.multiple_of(i * rt, rt)
        win = ext_ref[pl.ds(t0, rt + CTX_PAD + SUBLANES), :]
        shifted = [win[r:r + rt + CTX_PAD] for r in range(SUBLANES)]
        acc = jnp.zeros((rt, C), F32) + b_ref[...]
        for kk in range(CONV_WIDTH):
            a, r = divmod(first + kk, SUBLANES)
            acc = acc + w_ref[kk:kk + 1, :] * shifted[r][SUBLANES * a:SUBLANES * a + rt]
        mu = jnp.mean(acc, axis=-1, keepdims=True)
        xc = acc - mu
        var = jnp.mean(xc * xc, axis=-1, keepdims=True)
        y = xc * lax.rsqrt(var + EPS) * g_ref[...] + be_ref[...]
        o_ref[pl.ds(t0, rt), :] = _silu(y).astype(BF16)
        return carry

    lax.fori_loop(0, T // rt, tile, 0)


def _conv(glu, ctx, dw_w, dw_b, ln_g, ln_b):
    B, T, C = glu.shape
    rt = min(32, T)
    wpad = jnp.pad(dw_w, ((0, CTX_PAD - CONV_WIDTH), (0, 0)))
    kern = functools.partial(_conv_kernel, T=T, rt=rt)
    return pl.pallas_call(
        kern,
        grid=(B,),
        in_specs=[pl.BlockSpec((None, T, C), lambda b: (b, 0, 0)),
                  pl.BlockSpec((None, CTX_PAD, C), lambda b: (b, 0, 0)),
                  _const_spec((CTX_PAD, C)), _const_spec((1, C)), _const_spec((1, C)),
                  _const_spec((1, C))],
        out_specs=pl.BlockSpec((None, T, C), lambda b: (b, 0, 0)),
        out_shape=jax.ShapeDtypeStruct((B, T, C), BF16),
        scratch_shapes=[pltpu.VMEM((T + CTX_PAD + SUBLANES, C), F32)],
        compiler_params=_cparams(("parallel",)),
        name="conv",
    )(glu, ctx, wpad, dw_b.reshape(1, C), ln_g.reshape(1, C), ln_b.reshape(1, C))


def _outproj_kernel(attn_ref, conv_ref, x_ref, g1_ref, sh2_ref, sc2_ref, g2_ref, n2_ref,
                    woa_ref, wob_ref, bo_ref, wr_ref, wsg_ref, wsu_ref, wsd_ref,
                    base_ref, h2_ref, lg_ref):
    mix = (jnp.dot(attn_ref[...], woa_ref[...], preferred_element_type=F32)
           + jnp.dot(conv_ref[...], wob_ref[...], preferred_element_type=F32) + bo_ref[...])
    x1 = x_ref[...] + g1_ref[...] * mix
    y = x1 * lax.rsqrt(jnp.mean(x1 * x1, axis=-1, keepdims=True) + EPS) * n2_ref[...]
    h2 = y * (1.0 + sc2_ref[...]) + sh2_ref[...]
    h2_ref[...] = h2
    hb = h2.astype(BF16)
    lg_ref[...] = jnp.dot(hb, wr_ref[...], preferred_element_type=F32)
    sg = jnp.dot(hb, wsg_ref[...], preferred_element_type=F32)
    su = jnp.dot(hb, wsu_ref[...], preferred_element_type=F32)
    shared = jnp.dot((_silu(sg) * su).astype(BF16), wsd_ref[...], preferred_element_type=F32)
    base_ref[...] = x1 + g2_ref[...] * shared


def _outproj(attn, conv, x, g1, sh2, sc2, g2, n2, woa, wob, bo, wr, wsg, wsu, wsd, *, tm, per_token):
    Bx, T, D = x.shape
    E = wr.shape[1]
    Ds = wsg.shape[1]
    tok = lambda n: pl.BlockSpec((None, tm, n), lambda b, t: (b, t, 0))
    mod = lambda: _mod_spec(per_token, tm, D)
    return pl.pallas_call(
        _outproj_kernel,
        grid=(Bx, T // tm),
        in_specs=[tok(D_ATTN), tok(D - D_ATTN), tok(D), mod(), mod(), mod(), mod(),
                  _const_spec((1, D)), _const_spec((D_ATTN, D)), _const_spec((D - D_ATTN, D)),
                  _const_spec((1, D)), _const_spec((D, E)), _const_spec((D, Ds)),
                  _const_spec((D, Ds)), _const_spec((Ds, D))],
        out_specs=[tok(D), tok(D), tok(E)],
        out_shape=[jax.ShapeDtypeStruct((Bx, T, D), F32), jax.ShapeDtypeStruct((Bx, T, D), F32),
                   jax.ShapeDtypeStruct((Bx, T, E), F32)],
        compiler_params=_cparams(("parallel", "arbitrary")),
        name="outproj",
    )(attn, conv, x, g1, sh2, sc2, g2, n2, woa, wob, bo, wr, wsg, wsu, wsd)


def _route_kernel(lg_ref, bias_ref, idx_ref, w_ref):
    tm, E = lg_ref.shape
    gsz = E // N_GROUPS
    scores = jax.nn.sigmoid(lg_ref[...])
    biased = scores + bias_ref[...]
    lane = lax.broadcasted_iota(I32, (tm, E), 1).astype(F32)
    lane_i = lax.broadcasted_iota(I32, (tm, E), 1)
    in_group = lambda g: (lane_i >= g * gsz) & (lane_i < (g + 1) * gsz)
    ninf = -jnp.inf

    def first_argmax(vals):
        m = jnp.max(vals, axis=-1, keepdims=True)
        i = jnp.min(jnp.where(vals == m, lane, float(E)), axis=-1, keepdims=True)
        return m, i

    gs = []
    for g in range(N_GROUPS):
        mg = jnp.where(in_group(g), biased, ninf)
        m1, i1 = first_argmax(mg)
        m2 = jnp.max(jnp.where(lane == i1, ninf, mg), axis=-1, keepdims=True)
        gs.append(m1 + m2)
    keep = jnp.zeros((tm, E), jnp.bool_)
    for g in range(N_GROUPS):
        rank = jnp.zeros((tm, 1), F32)
        for o in range(N_GROUPS):
            if o == g:
                continue
            beats = (gs[o] > gs[g]) | ((gs[o] == gs[g]) & (o < g))
            rank = rank + jnp.where(beats, 1.0, 0.0)
        keep = keep | (in_group(g) & (rank < float(TOPK_GROUPS)))
    masked = jnp.where(keep, biased, ninf)
    out_lane = lax.broadcasted_iota(I32, (tm, LANES), 1)
    idx_out = jnp.zeros((tm, LANES), F32)
    w_out = jnp.zeros((tm, LANES), F32)
    for kk in range(TOP_K):
        _, i = first_argmax(masked)
        hit = lane == i
        wk = jnp.sum(jnp.where(hit, scores, 0.0), axis=-1, keepdims=True)
        masked = jnp.where(hit, ninf, masked)
        idx_out = jnp.where(out_lane == kk, i, idx_out)
        w_out = jnp.where(out_lane == kk, wk, w_out)
    w_out = w_out / jnp.sum(w_out, axis=-1, keepdims=True) * ROUTED_SCALE
    idx_ref[...] = idx_out.astype(I32)
    w_ref[...] = w_out


def _route(logits, bias):
    M, E = logits.shape
    tm = 256
    return pl.pallas_call(
        _route_kernel,
        grid=(M // tm,),
        in_specs=[pl.BlockSpec((tm, E), lambda i: (i, 0)), _const_spec((1, E))],
        out_specs=[pl.BlockSpec((tm, LANES), lambda i: (i, 0)), pl.BlockSpec((tm, LANES), lambda i: (i, 0))],
        out_shape=[jax.ShapeDtypeStruct((M, LANES), I32), jax.ShapeDtypeStruct((M, LANES), F32)],
        compiler_params=_cparams(("parallel",)),
        name="route",
    )(logits, bias.reshape(1, E))


def _moe_kernel(ce_ref, meta_ref, srcn_ref, srcc_ref, dst_ref, roww_ref, x_hbm, wg_ref, wu_ref, wd_ref,
                y_hbm, xbuf, ybuf, wgb, wub, wdb, src_s, dst_s, gsem, ssem, isem):
    c = pl.program_id(0)
    n_used = meta_ref[0]
    n_steps = pl.num_programs(0)
    slot = c % 2
    R = MOE_ROWS

    def gather_copy(tok, s, r):
        return pltpu.make_async_copy(x_hbm.at[pl.ds(tok, 1), :], xbuf.at[s, pl.ds(r, 1), :], gsem.at[s])

    def scatter_copy(row, s, r):
        return pltpu.make_async_copy(ybuf.at[s, pl.ds(r, 1), :], y_hbm.at[pl.ds(row, 1), :], ssem.at[s])

    def load_indices(vmem_ref, smem_ref):
        cp = pltpu.make_async_copy(vmem_ref, smem_ref, isem)
        cp.start()
        cp.wait()

    def issue_gather(idx_vmem, s):
        load_indices(idx_vmem, src_s)

        def body(r, carry):
            gather_copy(src_s[0, 0, r], s, r).start()
            return carry
        lax.fori_loop(0, R, body, 0, unroll=8)

    @pl.when(c == 0)
    def _():
        issue_gather(srcc_ref, 0)

    @pl.when(c + 1 < n_used)
    def _():
        issue_gather(srcn_ref, 1 - slot)

    @pl.when(c < n_used)
    def _():
        prev = jnp.maximum(c - 1, 0)

        @pl.when((c == 0) | (ce_ref[c] != ce_ref[prev]))
        def _():
            wgb[...] = wg_ref[...].astype(BF16)
            wub[...] = wu_ref[...].astype(BF16)
            wdb[...] = wd_ref[...].astype(BF16)

        pltpu.make_async_copy(x_hbm.at[pl.ds(0, R), :], xbuf.at[slot], gsem.at[slot]).wait()

        @pl.when(c >= 2)
        def _():
            pltpu.make_async_copy(ybuf.at[slot], y_hbm.at[pl.ds(0, R), :], ssem.at[slot]).wait()

        xb = xbuf[slot].astype(BF16)
        g = jnp.dot(xb, wgb[...], preferred_element_type=F32)
        u = jnp.dot(xb, wub[...], preferred_element_type=F32)
        y = jnp.dot((_silu(g) * u).astype(BF16), wdb[...], preferred_element_type=F32)
        ybuf[slot] = y * roww_ref[...]
        load_indices(dst_ref, dst_s)

        def body(r, carry):
            scatter_copy(dst_s[0, 0, r], slot, r).start()
            return carry
        lax.fori_loop(0, R, body, 0, unroll=8)

    @pl.when(c == n_steps - 1)
    def _():
        last = n_used - 1
        pltpu.make_async_copy(ybuf.at[last % 2], y_hbm.at[pl.ds(0, R), :], ssem.at[last % 2]).wait()

        @pl.when(n_used >= 2)
        def _():
            s = (last - 1) % 2
            pltpu.make_async_copy(ybuf.at[s], y_hbm.at[pl.ds(0, R), :], ssem.at[s]).wait()

        ybuf[0] = jnp.zeros(ybuf.shape[1:], F32)
        dump0 = y_hbm.shape[0] - 2 * R
        for s in range(2):
            cp = pltpu.make_async_copy(ybuf.at[0], y_hbm.at[pl.ds(dump0 + s * R, R), :], ssem.at[0])
            cp.start()
            cp.wait()


def _moe(h2, idx, gw, w_gate_e, w_up_e, w_down_e):
    M, D = h2.shape
    E, _, De = w_gate_e.shape
    R = MOE_ROWS
    A = M * TOP_K
    nch = A // R + E
    flat_e = idx.reshape(A)
    order = jnp.argsort(flat_e, stable=True).astype(I32)
    sorted_e = flat_e[order]
    experts = jnp.arange(E, dtype=I32)
    start = jnp.searchsorted(sorted_e, experts, side='left').astype(I32)
    counts = jnp.searchsorted(sorted_e, experts, side='right').astype(I32) - start
    nch_e = (counts + R - 1) // R
    ch_end = jnp.cumsum(nch_e).astype(I32)
    ch_start = ch_end - nch_e
    n_used = ch_end[-1]
    cid = jnp.arange(nch, dtype=I32)
    ce = jnp.minimum(jnp.searchsorted(ch_end, cid, side='right').astype(I32), E - 1)
    ce = jnp.where(cid < n_used, ce, ce[jnp.maximum(n_used - 1, 0)])
    rr = jnp.arange(R, dtype=I32)[None, :]
    local = (cid - ch_start[ce])[:, None] * R + rr
    valid = (cid[:, None] < n_used) & (local < counts[ce][:, None])
    a = order[jnp.clip(start[ce][:, None] + local, 0, A - 1)]
    src = jnp.where(valid, a // TOP_K, 0)
    dst = jnp.where(valid, (a % TOP_K) * M + a // TOP_K, TOP_K * M + (cid[:, None] % 2) * R + rr)
    roww = jnp.where(valid, gw.reshape(A)[a], 0.0)
    meta = jnp.stack([n_used, n_used]).astype(I32)

    idx_spec = lambda f: pl.BlockSpec((1, 1, R), f)
    wspec = lambda a_, b_: pl.BlockSpec((None, a_, b_), lambda c, ce_r, m_r: (ce_r[c], 0, 0))
    return pl.pallas_call(
        _moe_kernel,
        grid_spec=pltpu.PrefetchScalarGridSpec(
            num_scalar_prefetch=2,
            grid=(nch,),
            in_specs=[idx_spec(lambda c, ce_r, m_r: (jnp.minimum(c + 1, nch - 1), 0, 0)),
                      idx_spec(lambda c, ce_r, m_r: (c, 0, 0)),
                      idx_spec(lambda c, ce_r, m_r: (c, 0, 0)),
                      pl.BlockSpec((None, R, 1), lambda c, ce_r, m_r: (c, 0, 0)),
                      pl.BlockSpec(memory_space=pl.ANY),
                      wspec(D, De), wspec(D, De), wspec(De, D)],
            out_specs=pl.BlockSpec(memory_space=pl.ANY),
            scratch_shapes=[pltpu.VMEM((2, R, D), F32), pltpu.VMEM((2, R, D), F32),
                            pltpu.VMEM((D, De), BF16), pltpu.VMEM((D, De), BF16),
                            pltpu.VMEM((De, D), BF16),
                            pltpu.SMEM((1, 1, R), I32), pltpu.SMEM((1, 1, R), I32),
                            pltpu.SemaphoreType.DMA((2,)), pltpu.SemaphoreType.DMA((2,)),
                            pltpu.SemaphoreType.DMA(())]),
        out_shape=jax.ShapeDtypeStruct((TOP_K * M + 2 * R, D), F32),
        compiler_params=_cparams(("arbitrary",)),
        name="moe",
    )(ce, meta, src.reshape(nch, 1, R), src.reshape(nch, 1, R), dst.reshape(nch, 1, R),
      roww.reshape(nch, R, 1), h2, w_gate_e, w_up_e, w_down_e)


def _combine_kernel(base_ref, g2_ref, *refs):
    o_ref = refs[-1]
    routed = refs[0][...]
    for r in refs[1:-1]:
        routed = routed + r[...]
    o_ref[...] = base_ref[...] + g2_ref[...] * routed


def _combine(base, g2, y, *, m_total, tok0, tm, per_token):
    Bx, T, D = base.shape
    nt = T // tm

    def plane(kk):
        off = (kk * m_total + tok0) // tm
        return pl.BlockSpec((tm, D), lambda b, t: (off + b * nt + t, 0))

    return pl.pallas_call(
        _combine_kernel,
        grid=(Bx, nt),
        in_specs=[pl.BlockSpec((None, tm, D), lambda b, t: (b, t, 0)), _mod_spec(per_token, tm, D)]
                 + [plane(kk) for kk in range(TOP_K)],
        out_specs=pl.BlockSpec((None, tm, D), lambda b, t: (b, t, 0)),
        out_shape=jax.ShapeDtypeStruct((Bx, T, D), F32),
        compiler_params=_cparams(("parallel", "arbitrary")),
        name="combine",
    )(base, g2, *([y] * TOP_K))


def _repack_w_in(w_in, b_in):
    c_ki = 3 * D_ATTN + IDX_HEADS * IDX_DIM
    c_wi = c_ki + IDX_DIM
    c_uv = c_wi + IDX_HEADS

    def pack(a):
        z = lambda n: jnp.zeros(a.shape[:-1] + (n,), a.dtype)
        return jnp.concatenate([a[..., :c_wi], z(C_WI - C_KI - IDX_DIM), a[..., c_wi:c_uv],
                                z(C_UV - C_WI - IDX_HEADS), a[..., c_uv:]], axis=-1)

    return pack(w_in).astype(BF16), pack(b_in[None, :])


def _pad_rows(a, n):
    return jnp.pad(a, ((0, 0), (0, n - a.shape[1]), (0, 0)))


def kernel(x_prompt, x_sample, cache_k, cache_v, cache_kidx, state_conv, c_prompt, c_sample, norm1_g, norm2_g, w_ada, b_ada, w_in, b_in, q_norm_g, k_norm_g, idx_k_ln_g, idx_k_ln_b, dw_w, dw_b, conv_ln_g, conv_ln_b, w_out, b_out, w_router, router_bias, w_gate_e, w_up_e, w_down_e, w_sh_gate, w_sh_up, w_sh_down):
    B, T, D = x_prompt.shape
    Bd, Td, _ = x_sample.shape
    depth = w_in.shape[0]
    past = cache_k.shape[2]
    Ms, Mp = Bd * Td, B * T
    M = Mp + Ms
    C = D - D_ATTN
    topk_p = min(IDX_TOPK_MAX, T // 4)
    topk_s = min(IDX_TOPK_MAX, (past + Td) // 4)
    l_s = past + Td
    s_pad = -(-l_s // KC) * KC
    tm_p, tm_s = 256, 128
    xp, xs = x_prompt, x_sample.reshape(1, Ms, D)
    outs = [[] for _ in range(8)]
    for l in range(depth):
        mods = _ada(jnp.concatenate([c_prompt, c_sample], axis=0), w_ada[l], b_ada[l])
        mp = [m[:, None, :] for m in jnp.split(mods[:B], 6, axis=-1)]
        ms = [jnp.repeat(m, Td, axis=0)[None] for m in jnp.split(mods[B:], 6, axis=-1)]
        w2, b2 = _repack_w_in(w_in[l], b_in[l])
        row = lambda a: a.reshape(1, -1)
        proj_w = (row(norm1_g[l]), w2, b2, row(q_norm_g[l]), row(k_norm_g[l]),
                  row(idx_k_ln_g[l]), row(idx_k_ln_b[l]))
        woa, wob = w_out[l, :D_ATTN].astype(BF16), w_out[l, D_ATTN:].astype(BF16)
        out_w = (row(norm2_g[l]), woa, wob, row(b_out[l]), w_router[l].astype(BF16),
                 w_sh_gate[l].astype(BF16), w_sh_up[l].astype(BF16), w_sh_down[l].astype(BF16))

        q, k, kb, v, vb, qi, ki, kib, wi, glu = _inproj(xp, mp[0], mp[1], *proj_w, tm=tm_p, per_token=False)
        attn = _dsa(q, qi, wi, kb, vb, kib, q_base=0, l_valid=T, topk=topk_p, causal=True)
        conv = _conv(glu, jnp.zeros((B, CTX_PAD, C), F32), dw_w[l], dw_b[l], conv_ln_g[l], conv_ln_b[l])
        base_p, h2_p, lg_p = _outproj(attn, conv, xp, mp[2], mp[3], mp[4], mp[5], *out_w,
                                      tm=tm_p, per_token=False)
        outs[0].append(k.reshape(B, T, N_HEADS, HEAD_DIM))
        outs[1].append(v.reshape(B, T, N_HEADS, HEAD_DIM))
        outs[2].append(ki)
        outs[3].append(glu[:, T - CTX:])

        q, k, kb, v, vb, qi, ki, kib, wi, glu = _inproj(xs, ms[0], ms[1], *proj_w, tm=tm_s, per_token=True)
        per_b = lambda a: a.reshape(Bd, Td, a.shape[-1])
        cat = lambda cache, new: _pad_rows(
            jnp.concatenate([cache.reshape(Bd, past, -1).astype(BF16), per_b(new)], axis=1), s_pad)
        attn = _dsa(_pad_rows(per_b(q), QB), _pad_rows(per_b(qi), QB), _pad_rows(per_b(wi), QB),
                    cat(cache_k[l], kb), cat(cache_v[l], vb), cat(cache_kidx[l], kib),
                    q_base=past, l_valid=l_s, topk=topk_s, causal=False)
        attn = attn[:, :Td].reshape(1, Ms, D_ATTN)
        ext = jnp.concatenate([state_conv[l], per_b(glu)], axis=1)
        ctx = jnp.pad(state_conv[l], ((0, 0), (CTX_PAD - CTX, 0), (0, 0)))
        conv = _conv(per_b(glu), ctx, dw_w[l], dw_b[l], conv_ln_g[l], conv_ln_b[l]).reshape(1, Ms, C)
        base_s, h2_s, lg_s = _outproj(attn, conv, xs, ms[2], ms[3], ms[4], ms[5], *out_w,
                                      tm=tm_s, per_token=True)
        outs[4].append(per_b(k).reshape(Bd, Td, N_HEADS, HEAD_DIM))
        outs[5].append(per_b(v).reshape(Bd, Td, N_HEADS, HEAD_DIM))
        outs[6].append(per_b(ki))
        outs[7].append(ext[:, Td:])

        h2 = jnp.concatenate([h2_p.reshape(Mp, D), h2_s.reshape(Ms, D)], axis=0)
        lg = jnp.concatenate([lg_p.reshape(Mp, -1), lg_s.reshape(Ms, -1)], axis=0)
        idx, gw = _route(lg, router_bias[l])
        y = _moe(h2, idx[:, :TOP_K], gw[:, :TOP_K], w_gate_e[l], w_up_e[l], w_down_e[l])
        xp = _combine(base_p, mp[5], y, m_total=M, tok0=0, tm=tm_p, per_token=False)
        xs = _combine(base_s, ms[5], y, m_total=M, tok0=Mp, tm=256, per_token=True)
    return (xp, xs.reshape(Bd, Td, D), *[jnp.stack(o) for o in outs])
```

```python
import functools

import numpy as np
import jax
import jax.numpy as jnp
from jax import lax
from jax.experimental import pallas as pl
from jax.experimental.pallas import tpu as pltpu

F32, BF16, I32 = jnp.float32, jnp.bfloat16, jnp.int32

N_HEADS = 8
HEAD_DIM = 128
D_ATTN = N_HEADS * HEAD_DIM
IDX_HEADS = 16
IDX_DIM = 64
CHUNK = 64
CHUNK_SHIFT = 6
CONV_WIDTH = 31
CTX = CONV_WIDTH - 1
IDX_TOPK_MAX = 256
N_GROUPS = 8
TOPK_GROUPS = 4
TOP_K = 8
ROUTED_SCALE = 2.5
EPS = 1e-6
NEG_INF = -1e30

LANES = 128
SUBLANES = 8
VMEM_LIMIT = 56 * 1024 * 1024

QB = 128
KC = 256
CTX_PAD = 32
MOE_ROWS = 256
ROW_TILES = 16


def _cparams(sem):
    return pltpu.CompilerParams(dimension_semantics=sem, vmem_limit_bytes=VMEM_LIMIT)


def _const_spec(shape):
    nd = len(shape)
    return pl.BlockSpec(shape, lambda *_: (0,) * nd, pipeline_mode=pl.Buffered(1))


def _silu(x):
    return x * jax.nn.sigmoid(x)


def _tree_sum(parts):
    while len(parts) > 1:
        parts = [a + b for a, b in zip(parts[::2], parts[1::2])] + parts[len(parts) & ~1:]
    return parts[0]


def _ada_kernel(c_ref, w_ref, b_ref, o_ref):
    a = _silu(c_ref[...]).astype(BF16)
    o_ref[...] = jnp.dot(a, w_ref[...].astype(BF16), preferred_element_type=F32) + b_ref[...]


def _ada(c_all, w_ada, b_ada):
    R, D = c_all.shape
    N = w_ada.shape[1]
    TN = 1024
    return pl.pallas_call(
        _ada_kernel,
        grid=(N // TN,),
        in_specs=[pl.BlockSpec((R, D), lambda n: (0, 0)),
                  pl.BlockSpec((D, TN), lambda n: (0, n)),
                  pl.BlockSpec((1, TN), lambda n: (0, n))],
        out_specs=pl.BlockSpec((R, TN), lambda n: (0, n)),
        out_shape=jax.ShapeDtypeStruct((R, N), F32),
        compiler_params=_cparams(("arbitrary",)),
        name="ada",
    )(c_all, w_ada, b_ada.reshape(1, N))


C_Q, C_K, C_V, C_QI = 0, 1024, 2048, 3072
C_KI, C_WI, C_UV, C_UG, C_END = 4096, 4224, 4352, 5376, 6400


def _inproj_kernel(x_ref, sh_ref, sc_ref, g_ref, w_ref, b_ref, qg_ref, kg_ref, lg_ref, lb_ref,
                   q_ref, k_ref, kb_ref, v_ref, vb_ref, qi_ref, ki_ref, kib_ref, wi_ref, glu_ref):
    x = x_ref[...]
    y = x * lax.rsqrt(jnp.mean(x * x, axis=-1, keepdims=True) + EPS) * g_ref[...]
    hb = (y * (1.0 + sc_ref[...]) + sh_ref[...]).astype(BF16)

    def proj(lo, hi):
        return jnp.dot(hb, w_ref[:, lo:hi], preferred_element_type=F32) + b_ref[:, lo:hi]

    def head_norm(z, gain):
        return z * lax.rsqrt(jnp.mean(z * z, axis=-1, keepdims=True) + EPS) * gain

    zq = proj(C_Q, C_K)
    zk = proj(C_K, C_V)
    for h in range(N_HEADS):
        sl = slice(h * HEAD_DIM, (h + 1) * HEAD_DIM)
        q_ref[:, sl] = head_norm(zq[:, sl], qg_ref[...]).astype(BF16)
        kn = head_norm(zk[:, sl], kg_ref[...])
        k_ref[:, sl] = kn
        kb_ref[:, sl] = kn.astype(BF16)
    zv = proj(C_V, C_QI)
    v_ref[...] = zv
    vb_ref[...] = zv.astype(BF16)
    qi_ref[...] = (proj(C_QI, C_KI) * (IDX_DIM ** -0.5)).astype(BF16)
    zki = proj(C_KI, C_WI)[:, :IDX_DIM]
    mu = jnp.mean(zki, axis=-1, keepdims=True)
    xc = zki - mu
    var = jnp.mean(xc * xc, axis=-1, keepdims=True)
    ki = xc * lax.rsqrt(var + EPS) * lg_ref[...] + lb_ref[...]
    ki_ref[...] = ki
    kib_ref[...] = ki.astype(BF16)
    wi_ref[...] = proj(C_WI, C_UV)[:, :IDX_HEADS] * (IDX_HEADS ** -0.5)
    glu_ref[...] = proj(C_UV, C_UG) * jax.nn.sigmoid(proj(C_UG, C_END))


def _mod_spec(per_token, tm, d):
    if per_token:
        return pl.BlockSpec((None, tm, d), lambda b, t: (b, t, 0))
    return pl.BlockSpec((None, 1, d), lambda b, t: (b, 0, 0))


def _inproj(x, sh, sc, g, w2, b2, qg, kg, lg, lb, *, tm, per_token):
    Bx, T, D = x.shape
    tok = lambda n: pl.BlockSpec((None, tm, n), lambda b, t: (b, t, 0))
    shp = lambda n, dt: jax.ShapeDtypeStruct((Bx, T, n), dt)
    return pl.pallas_call(
        _inproj_kernel,
        grid=(Bx, T // tm),
        in_specs=[tok(D), _mod_spec(per_token, tm, D), _mod_spec(per_token, tm, D),
                  _const_spec((1, D)), _const_spec((D, C_END)), _const_spec((1, C_END)),
                  _const_spec((1, HEAD_DIM)), _const_spec((1, HEAD_DIM)),
                  _const_spec((1, IDX_DIM)), _const_spec((1, IDX_DIM))],
        out_specs=[tok(D_ATTN), tok(D_ATTN), tok(D_ATTN), tok(D_ATTN), tok(D_ATTN),
                   tok(IDX_HEADS * IDX_DIM), tok(IDX_DIM), tok(IDX_DIM), tok(IDX_HEADS),
                   tok(D - D_ATTN)],
        out_shape=[shp(D_ATTN, BF16), shp(D_ATTN, F32), shp(D_ATTN, BF16), shp(D_ATTN, F32),
                   shp(D_ATTN, BF16), shp(IDX_HEADS * IDX_DIM, BF16), shp(IDX_DIM, F32),
                   shp(IDX_DIM, BF16), shp(IDX_HEADS, F32), shp(D - D_ATTN, F32)],
        compiler_params=_cparams(("parallel", "arbitrary")),
        name="inproj",
    )(x, sh, sc, g, w2, b2, qg, kg, lg, lb)


def _alibi_slopes():
    return [float(np.float32(2.0) ** np.float32(-8.0 * h / N_HEADS)) for h in range(1, N_HEADS + 1)]


def _dsa_kernel(qT_ref, qiw_ref, wiw_ref, k_ref, vT_ref, ki_ref, o_ref,
                key_ref, sel_ref, acc_ref, m_ref, l_ref, *, q_base, l_valid, topk, causal, n_kc):
    jb = pl.program_id(1)
    nkc = (jb * QB + QB + KC - 1) // KC if causal else n_kc
    lane = lax.broadcasted_iota(I32, (1, QB), 1)
    qpos = q_base + jb * QB + lane
    qchunk = qpos >> CHUNK_SHIFT
    sub = lax.broadcasted_iota(I32, (KC, 1), 0)

    def visible(c):
        kpos = c * KC + sub
        return ((kpos >> CHUNK_SHIFT) <= qchunk) & (kpos < l_valid)

    def score_chunk(c, carry):
        kic = ki_ref[c]
        acc = jnp.zeros((KC, QB), F32)
        for hp in range(IDX_HEADS // 2):
            sl = slice(hp * 2 * QB, (hp + 1) * 2 * QB)
            r = jnp.dot(kic, qiw_ref[:, sl], preferred_element_type=F32)
            t = jnp.maximum(r, 0.0) * wiw_ref[:, sl]
            acc = acc + t[:, :QB] + t[:, QB:]
        bits = pltpu.bitcast(jnp.where(visible(c), acc, -jnp.inf), I32)
        key_ref[c] = bits ^ ((bits >> 31) & 0x7FFFFFFF)
        return carry

    lax.fori_loop(0, nkc, score_chunk, 0)

    def count(pred):
        def body(c, cnt):
            hit = jnp.where(pred(key_ref[c]), 1.0, 0.0)
            return cnt + _tree_sum([hit[i:i + SUBLANES] for i in range(0, KC, SUBLANES)])
        cnt = lax.fori_loop(0, nkc, body, jnp.zeros((SUBLANES, QB), F32))
        return jnp.sum(cnt, axis=0, keepdims=True)

    kf = float(topk)
    int_min = jnp.full((1, QB), -2 ** 31, I32)
    zero = jnp.zeros((1, QB), I32)
    thr0 = jnp.where(count(lambda key: key >= zero) >= kf, zero, int_min)

    def bisect(i, lo):
        cand = lo | jnp.left_shift(jnp.int32(1), 30 - i)
        return jnp.where(count(lambda key: key >= cand) >= kf, cand, lo)

    thr = lax.fori_loop(0, 31, bisect, thr0)
    need = kf - count(lambda key: key > thr)

    tri = (lax.broadcasted_iota(I32, (KC, KC), 0) >= lax.broadcasted_iota(I32, (KC, KC), 1))
    tri = jnp.where(tri, 1.0, 0.0).astype(BF16)

    def select_chunk(c, carry):
        key = key_ref[c]
        tie = key == thr
        tie_f = jnp.where(tie, 1.0, 0.0)
        pref = jnp.dot(tri, tie_f.astype(BF16), preferred_element_type=F32) + carry
        sel = ((key > thr) | (tie & (pref <= need))) & visible(c)
        sel_ref[c] = jnp.where(sel, 1.0, 0.0)
        return carry + jnp.sum(tie_f, axis=0, keepdims=True)

    lax.fori_loop(0, nkc, select_chunk, jnp.zeros((1, QB), F32))

    m_ref[...] = jnp.full((N_HEADS, QB), 0.1 * NEG_INF, F32)
    l_ref[...] = jnp.zeros((N_HEADS, QB), F32)
    acc_ref[...] = jnp.zeros((D_ATTN, QB), F32)
    scale = HEAD_DIM ** -0.5
    slopes = [s / scale for s in _alibi_slopes()]
    c2 = scale * float(np.log2(np.e))

    def attend_chunk(c, carry):
        dist = jnp.abs(qpos - (c * KC + sub)).astype(F32)
        selc = sel_ref[c] > 0.5
        for h in range(N_HEADS):
            hs = slice(h * HEAD_DIM, (h + 1) * HEAD_DIM)
            lt = jnp.dot(k_ref[c, :, hs], qT_ref[hs, :], preferred_element_type=F32)
            lt = jnp.where(selc, lt - slopes[h] * dist, NEG_INF)
            m_old = m_ref[h:h + 1, :]
            m_new = jnp.maximum(m_old, jnp.max(lt, axis=0, keepdims=True))
            alpha = jnp.exp2((m_old - m_new) * c2)
            p = jnp.exp2((lt - m_new) * c2)
            l_ref[h:h + 1, :] = alpha * l_ref[h:h + 1, :] + jnp.sum(p, axis=0, keepdims=True)
            pv = jnp.dot(vT_ref[c, hs, :], p.astype(BF16), preferred_element_type=F32)
            acc_ref[hs, :] = acc_ref[hs, :] * alpha + pv
            m_ref[h:h + 1, :] = m_new
        return carry

    lax.fori_loop(0, nkc, attend_chunk, 0)

    for h in range(N_HEADS):
        hs = slice(h * HEAD_DIM, (h + 1) * HEAD_DIM)
        o = acc_ref[hs, :] / l_ref[h:h + 1, :]
        o_ref[:, hs] = o.T.astype(BF16)


def _dsa(q, qi, wi, k, v, ki, *, q_base, l_valid, topk, causal):
    B, Tq, _ = q.shape
    S = k.shape[1]
    nj, n_kc = Tq // QB, S // KC
    qT = q.transpose(0, 2, 1)
    qiw = qi.reshape(B, nj, QB, IDX_HEADS, IDX_DIM).transpose(0, 1, 4, 3, 2)
    qiw = qiw.reshape(B, nj, IDX_DIM, IDX_HEADS * QB)
    wiw = wi.reshape(B, nj, QB, IDX_HEADS).transpose(0, 1, 3, 2).reshape(B, nj, 1, IDX_HEADS * QB)
    kc = k.reshape(B, n_kc, KC, D_ATTN)
    vT = v.reshape(B, n_kc, KC, D_ATTN).transpose(0, 1, 3, 2)
    kic = ki.reshape(B, n_kc, KC, IDX_DIM)
    kern = functools.partial(_dsa_kernel, q_base=q_base, l_valid=l_valid, topk=topk,
                             causal=causal, n_kc=n_kc)
    return pl.pallas_call(
        kern,
        grid=(B, nj),
        in_specs=[pl.BlockSpec((None, D_ATTN, QB), lambda b, j: (b, 0, j)),
                  pl.BlockSpec((None, None, IDX_DIM, IDX_HEADS * QB), lambda b, j: (b, j, 0, 0)),
                  pl.BlockSpec((None, None, 1, IDX_HEADS * QB), lambda b, j: (b, j, 0, 0)),
                  pl.BlockSpec((None, n_kc, KC, D_ATTN), lambda b, j: (b, 0, 0, 0)),
                  pl.BlockSpec((None, n_kc, D_ATTN, KC), lambda b, j: (b, 0, 0, 0)),
                  pl.BlockSpec((None, n_kc, KC, IDX_DIM), lambda b, j: (b, 0, 0, 0))],
        out_specs=pl.BlockSpec((None, QB, D_ATTN), lambda b, j: (b, j, 0)),
        out_shape=jax.ShapeDtypeStruct((B, Tq, D_ATTN), BF16),
        scratch_shapes=[pltpu.VMEM((n_kc, KC, QB), I32), pltpu.VMEM((n_kc, KC, QB), F32),
                        pltpu.VMEM((D_ATTN, QB), F32), pltpu.VMEM((N_HEADS, QB), F32),
                        pltpu.VMEM((N_HEADS, QB), F32)],
        compiler_params=_cparams(("parallel", "arbitrary")),
        name="dsa",
    )(qT, qiw, wiw, kc, vT, kic)


def _conv_kernel(glu_ref, ctx_ref, w_ref, b_ref, g_ref, be_ref, o_ref, ext_ref, *, T, rt):
    C = glu_ref.shape[-1]
    ext_ref[0:CTX_PAD, :] = ctx_ref[...]
    ext_ref[CTX_PAD:CTX_PAD + T, :] = glu_ref[...]
    ext_ref[CTX_PAD + T:, :] = jnp.zeros((SUBLANES, C), F32)
    first = CTX_PAD - CTX

    def tile(i, carry):
        t0 = pl.multiple_of(i * rt, rt)
        win = ext_ref[pl.ds(t0, rt + CTX_PAD + SUBLANES), :]
        shifted = [win[r:r + rt + CTX_PAD] for r in range(SUBLANES)]
        acc = jnp.zeros((rt, C), F32) + b_ref[...]
        for kk in range(CONV_WIDTH):
            a, r = divmod(first + kk, SUBLANES)
            acc = acc + w_ref[kk:kk + 1, :] * shifted[r][SUBLANES * a:SUBLANES * a + rt]
        mu = jnp.mean(acc, axis=-1, keepdims=True)
        xc = acc - mu
        var = jnp.mean(xc * xc, axis=-1, keepdims=True)
        y = xc * lax.rsqrt(var + EPS) * g_ref[...] + be_ref[...]
        o_ref[pl.ds(t0, rt), :] = _silu(y).astype(BF16)
        return carry

    lax.fori_loop(0, T // rt, tile, 0)


def _conv(glu, ctx, dw_w, dw_b, ln_g, ln_b):
    B, T, C = glu.shape
    rt = min(32, T)
    wpad = jnp.pad(dw_w, ((0, CTX_PAD - CONV_WIDTH), (0, 0)))
    kern = functools.partial(_conv_kernel, T=T, rt=rt)
    return pl.pallas_call(
        kern,
        grid=(B,),
        in_specs=[pl.BlockSpec((None, T, C), lambda b: (b, 0, 0)),
                  pl.BlockSpec((None, CTX_PAD, C), lambda b: (b, 0, 0)),
                  _const_spec((CTX_PAD, C)), _const_spec((1, C)), _const_spec((1, C)),
                  _const_spec((1, C))],
        out_specs=pl.BlockSpec((None, T, C), lambda b: (b, 0, 0)),
        out_shape=jax.ShapeDtypeStruct((B, T, C), BF16),
        scratch_shapes=[pltpu.VMEM((T + CTX_PAD + SUBLANES, C), F32)],
        compiler_params=_cparams(("parallel",)),
        name="conv",
    )(glu, ctx, wpad, dw_b.reshape(1, C), ln_g.reshape(1, C), ln_b.reshape(1, C))


def _outproj_kernel(attn_ref, conv_ref, x_ref, g1_ref, sh2_ref, sc2_ref, g2_ref, n2_ref,
                    woa_ref, wob_ref, bo_ref, wr_ref, wsg_ref, wsu_ref, wsd_ref,
                    base_ref, h2_ref, lg_ref):
    mix = (jnp.dot(attn_ref[...], woa_ref[...], preferred_element_type=F32)
           + jnp.dot(conv_ref[...], wob_ref[...], preferred_element_type=F32) + bo_ref[...])
    x1 = x_ref[...] + g1_ref[...] * mix
    y = x1 * lax.rsqrt(jnp.mean(x1 * x1, axis=-1, keepdims=True) + EPS) * n2_ref[...]
    h2 = y * (1.0 + sc2_ref[...]) + sh2_ref[...]
    h2_ref[...] = h2
    hb = h2.astype(BF16)
    lg_ref[...] = jnp.dot(hb, wr_ref[...], preferred_element_type=F32)
    sg = jnp.dot(hb, wsg_ref[...], preferred_element_type=F32)
    su = jnp.dot(hb, wsu_ref[...], preferred_element_type=F32)
    shared = jnp.dot((_silu(sg) * su).astype(BF16), wsd_ref[...], preferred_element_type=F32)
    base_ref[...] = x1 + g2_ref[...] * shared


def _outproj(attn, conv, x, g1, sh2, sc2, g2, n2, woa, wob, bo, wr, wsg, wsu, wsd, *, tm, per_token):
    Bx, T, D = x.shape
    E = wr.shape[1]
    Ds = wsg.shape[1]
    tok = lambda n: pl.BlockSpec((None, tm, n), lambda b, t: (b, t, 0))
    mod = lambda: _mod_spec(per_token, tm, D)
    return pl.pallas_call(
        _outproj_kernel,
        grid=(Bx, T // tm),
        in_specs=[tok(D_ATTN), tok(D - D_ATTN), tok(D), mod(), mod(), mod(), mod(),
                  _const_spec((1, D)), _const_spec((D_ATTN, D)), _const_spec((D - D_ATTN, D)),
                  _const_spec((1, D)), _const_spec((D, E)), _const_spec((D, Ds)),
                  _const_spec((D, Ds)), _const_spec((Ds, D))],
        out_specs=[tok(D), tok(D), tok(E)],
        out_shape=[jax.ShapeDtypeStruct((Bx, T, D), F32), jax.ShapeDtypeStruct((Bx, T, D), F32),
                   jax.ShapeDtypeStruct((Bx, T, E), F32)],
        compiler_params=_cparams(("parallel", "arbitrary")),
        name="outproj",
    )(attn, conv, x, g1, sh2, sc2, g2, n2, woa, wob, bo, wr, wsg, wsu, wsd)


def _route_kernel(lg_ref, bias_ref, idx_ref, w_ref):
    tm, E = lg_ref.shape
    gsz = E // N_GROUPS
    scores = jax.nn.sigmoid(lg_ref[...])
    biased = scores + bias_ref[...]
    lane = lax.broadcasted_iota(I32, (tm, E), 1).astype(F32)
    lane_i = lax.broadcasted_iota(I32, (tm, E), 1)
    in_group = lambda g: (lane_i >= g * gsz) & (lane_i < (g + 1) * gsz)
    ninf = -jnp.inf

    def first_argmax(vals):
        m = jnp.max(vals, axis=-1, keepdims=True)
        i = jnp.min(jnp.where(vals == m, lane, float(E)), axis=-1, keepdims=True)
        return m, i

    gs = []
    for g in range(N_GROUPS):
        mg = jnp.where(in_group(g), biased, ninf)
        m1, i1 = first_argmax(mg)
        m2 = jnp.max(jnp.where(lane == i1, ninf, mg), axis=-1, keepdims=True)
        gs.append(m1 + m2)
    keep = jnp.zeros((tm, E), jnp.bool_)
    for g in range(N_GROUPS):
        rank = jnp.zeros((tm, 1), F32)
        for o in range(N_GROUPS):
            if o == g:
                continue
            beats = (gs[o] > gs[g]) | ((gs[o] == gs[g]) & (o < g))
            rank = rank + jnp.where(beats, 1.0, 0.0)
        keep = keep | (in_group(g) & (rank < float(TOPK_GROUPS)))
    masked = jnp.where(keep, biased, ninf)
    out_lane = lax.broadcasted_iota(I32, (tm, LANES), 1)
    idx_out = jnp.zeros((tm, LANES), F32)
    w_out = jnp.zeros((tm, LANES), F32)
    for kk in range(TOP_K):
        _, i = first_argmax(masked)
        hit = lane == i
        wk = jnp.sum(jnp.where(hit, scores, 0.0), axis=-1, keepdims=True)
        masked = jnp.where(hit, ninf, masked)
        idx_out = jnp.where(out_lane == kk, i, idx_out)
        w_out = jnp.where(out_lane == kk, wk, w_out)
    w_out = w_out / jnp.sum(w_out, axis=-1, keepdims=True) * ROUTED_SCALE
    idx_ref[...] = idx_out.astype(I32)
    w_ref[...] = w_out


def _route(logits, bias):
    M, E = logits.shape
    tm = 256
    return pl.pallas_call(
        _route_kernel,
        grid=(M // tm,),
        in_specs=[pl.BlockSpec((tm, E), lambda i: (i, 0)), _const_spec((1, E))],
        out_specs=[pl.BlockSpec((tm, LANES), lambda i: (i, 0)), pl.BlockSpec((tm, LANES), lambda i: (i, 0))],
        out_shape=[jax.ShapeDtypeStruct((M, LANES), I32), jax.ShapeDtypeStruct((M, LANES), F32)],
        compiler_params=_cparams(("parallel",)),
        name="route",
    )(logits, bias.reshape(1, E))


def _moe_kernel(ce_ref, meta_ref, srcn_ref, srcc_ref, dst_ref, roww_ref, x_hbm, wg_ref, wu_ref, wd_ref,
                y_hbm, xbuf, ybuf, zbuf, wgb, wub, wdb, src_s, dst_s, gsem, ssem, isem):
    c = pl.program_id(0)
    n_used = meta_ref[0]
    n_steps = pl.num_programs(0)
    slot = c % 2
    R = MOE_ROWS

    def gather_copy(tok, s, i, u):
        return pltpu.make_async_copy(x_hbm.at[tok], xbuf.at[s, i, pl.ds(u, 1), :], gsem.at[s])

    def scatter_copy(row, s, i, u):
        return pltpu.make_async_copy(ybuf.at[s, i, pl.ds(u, 1), :], y_hbm.at[row], ssem.at[s])

    def whole_buffer_wait(buf, sem, s):
        pltpu.make_async_copy(buf.at[1 - s], buf.at[s], sem.at[s]).wait()

    def src_copy(vmem_ref):
        return pltpu.make_async_copy(vmem_ref, src_s, isem.at[0])

    def dst_copy():
        return pltpu.make_async_copy(dst_ref, dst_s, isem.at[1])

    def for_rows(fn):
        def body(i, carry):
            for u in range(SUBLANES):
                fn(i, u)
            return carry
        lax.fori_loop(0, R // SUBLANES, body, 0)

    def issue_gather(s):
        for_rows(lambda i, u: gather_copy(src_s[0, 0, i * SUBLANES + u], s, i, u).start())

    @pl.when(c == 0)
    def _():
        cp = src_copy(srcc_ref)
        cp.start()
        cp.wait()
        issue_gather(0)

    src_copy(srcn_ref).start()
    dst_copy().start()

    @pl.when(c < n_used)
    def _():
        prev = jnp.maximum(c - 1, 0)

        @pl.when((c == 0) | (ce_ref[c] != ce_ref[prev]))
        def _():
            wgb[...] = wg_ref[...].astype(BF16)
            wub[...] = wu_ref[...].astype(BF16)
            wdb[...] = wd_ref[...].astype(BF16)

        whole_buffer_wait(xbuf, gsem, slot)

        @pl.when(c >= 2)
        def _():
            whole_buffer_wait(ybuf, ssem, slot)

    src_copy(srcn_ref).wait()
    dst_copy().wait()

    @pl.when(c + 1 < n_used)
    def _():
        issue_gather(1 - slot)

    @pl.when(c < n_used)
    def _():
        D = xbuf.shape[-1]
        xb = xbuf[slot].reshape(R, D).astype(BF16)
        g = jnp.dot(xb, wgb[...], preferred_element_type=F32)
        u = jnp.dot(xb, wub[...], preferred_element_type=F32)
        y = jnp.dot((_silu(g) * u).astype(BF16), wdb[...], preferred_element_type=F32)
        ybuf[slot] = (y * roww_ref[...]).reshape(R // SUBLANES, SUBLANES, D)
        for_rows(lambda i, u: scatter_copy(dst_s[0, 0, i * SUBLANES + u], slot, i, u).start())

    @pl.when(c == n_steps - 1)
    def _():
        last = n_used - 1
        whole_buffer_wait(ybuf, ssem, last % 2)

        @pl.when(n_used >= 2)
        def _():
            whole_buffer_wait(ybuf, ssem, (last - 1) % 2)

        zbuf[...] = jnp.zeros(zbuf.shape, F32)
        dump0 = y_hbm.shape[0] - 2 * R
        for s in range(2):
            cp = pltpu.make_async_copy(zbuf, y_hbm.at[pl.ds(dump0 + s * R, R), 0], ssem.at[0])
            cp.start()
            cp.wait()


def _moe(h2, idx, gw, w_gate_e, w_up_e, w_down_e):
    M, D = h2.shape
    E, _, De = w_gate_e.shape
    R = MOE_ROWS
    A = M * TOP_K
    nch = A // R + E
    flat_e = idx.reshape(A)
    order = jnp.argsort(flat_e, stable=True).astype(I32)
    experts = jnp.arange(E, dtype=I32)
    counts = jnp.sum((flat_e[None, :] == experts[:, None]).astype(I32), axis=1)
    start = jnp.cumsum(counts).astype(I32) - counts
    nch_e = (counts + R - 1) // R
    ch_end = jnp.cumsum(nch_e).astype(I32)
    ch_start = ch_end - nch_e
    n_used = ch_end[-1]
    cid = jnp.arange(nch, dtype=I32)
    ce = jnp.sum((ch_end[None, :] <= jnp.minimum(cid, n_used - 1)[:, None]).astype(I32), axis=1)
    ce = jnp.minimum(ce, E - 1)
    rr = jnp.arange(R, dtype=I32)[None, :]
    local = (cid - ch_start[ce])[:, None] * R + rr
    valid = (cid[:, None] < n_used) & (local < counts[ce][:, None])
    a = order[jnp.clip(start[ce][:, None] + local, 0, A - 1)]
    src = jnp.where(valid, a // TOP_K, 0)
    dst = jnp.where(valid, (a % TOP_K) * M + a // TOP_K, TOP_K * M + (cid[:, None] % 2) * R + rr)
    roww = jnp.where(valid, gw.reshape(A)[a], 0.0)
    meta = jnp.stack([n_used, n_used]).astype(I32)
    h2 = h2.reshape(M, 1, D)

    idx_spec = lambda f: pl.BlockSpec((1, 1, R), f)
    wspec = lambda a_, b_: pl.BlockSpec((None, a_, b_), lambda c, ce_r, m_r: (ce_r[c], 0, 0))
    return pl.pallas_call(
        _moe_kernel,
        grid_spec=pltpu.PrefetchScalarGridSpec(
            num_scalar_prefetch=2,
            grid=(nch,),
            in_specs=[idx_spec(lambda c, ce_r, m_r: (jnp.minimum(c + 1, nch - 1), 0, 0)),
                      idx_spec(lambda c, ce_r, m_r: (c, 0, 0)),
                      idx_spec(lambda c, ce_r, m_r: (c, 0, 0)),
                      pl.BlockSpec((None, R, 1), lambda c, ce_r, m_r: (c, 0, 0)),
                      pl.BlockSpec(memory_space=pl.ANY),
                      wspec(D, De), wspec(D, De), wspec(De, D)],
            out_specs=pl.BlockSpec(memory_space=pl.ANY),
            scratch_shapes=[pltpu.VMEM((2, R // SUBLANES, SUBLANES, D), F32),
                            pltpu.VMEM((2, R // SUBLANES, SUBLANES, D), F32), pltpu.VMEM((R, D), F32),
                            pltpu.VMEM((D, De), BF16), pltpu.VMEM((D, De), BF16),
                            pltpu.VMEM((De, D), BF16),
                            pltpu.SMEM((1, 1, R), I32), pltpu.SMEM((1, 1, R), I32),
                            pltpu.SemaphoreType.DMA((2,)), pltpu.SemaphoreType.DMA((2,)),
                            pltpu.SemaphoreType.DMA((2,))]),
        out_shape=jax.ShapeDtypeStruct((TOP_K * M + 2 * R, 1, D), F32),
        compiler_params=_cparams(("arbitrary",)),
        name="moe",
    )(ce, meta, src.reshape(nch, 1, R), src.reshape(nch, 1, R), dst.reshape(nch, 1, R),
      roww.reshape(nch, R, 1), h2, w_gate_e, w_up_e, w_down_e)


def _combine_kernel(base_ref, g2_ref, *refs):
    o_ref = refs[-1]
    routed = _tree_sum([r[...] for r in refs[:-1]])
    o_ref[...] = base_ref[...] + g2_ref[...] * routed


def _combine(base, g2, y, *, m_total, tok0, tm, per_token):
    Bx, T, D = base.shape
    nt = T // tm

    def plane(kk):
        off = (kk * m_total + tok0) // tm
        return pl.BlockSpec((tm, None, D), lambda b, t: (off + b * nt + t, 0, 0))

    return pl.pallas_call(
        _combine_kernel,
        grid=(Bx, nt),
        in_specs=[pl.BlockSpec((None, tm, D), lambda b, t: (b, t, 0)), _mod_spec(per_token, tm, D)]
                 + [plane(kk) for kk in range(TOP_K)],
        out_specs=pl.BlockSpec((None, tm, D), lambda b, t: (b, t, 0)),
        out_shape=jax.ShapeDtypeStruct((Bx, T, D), F32),
        compiler_params=_cparams(("parallel", "arbitrary")),
        name="combine",
    )(base, g2, *([y] * TOP_K))


def _repack_w_in(w_in, b_in):
    c_ki = 3 * D_ATTN + IDX_HEADS * IDX_DIM
    c_wi = c_ki + IDX_DIM
    c_uv = c_wi + IDX_HEADS

    def pack(a):
        z = lambda n: jnp.zeros(a.shape[:-1] + (n,), a.dtype)
        return jnp.concatenate([a[..., :c_wi], z(C_WI - C_KI - IDX_DIM), a[..., c_wi:c_uv],
                                z(C_UV - C_WI - IDX_HEADS), a[..., c_uv:]], axis=-1)

    return pack(w_in).astype(BF16), pack(b_in[None, :])


def _pad_rows(a, n):
    return jnp.pad(a, ((0, 0), (0, n - a.shape[1]), (0, 0)))


def kernel(x_prompt, x_sample, cache_k, cache_v, cache_kidx, state_conv, c_prompt, c_sample, norm1_g, norm2_g, w_ada, b_ada, w_in, b_in, q_norm_g, k_norm_g, idx_k_ln_g, idx_k_ln_b, dw_w, dw_b, conv_ln_g, conv_ln_b, w_out, b_out, w_router, router_bias, w_gate_e, w_up_e, w_down_e, w_sh_gate, w_sh_up, w_sh_down):
    B, T, D = x_prompt.shape
    Bd, Td, _ = x_sample.shape
    depth = w_in.shape[0]
    past = cache_k.shape[2]
    Ms, Mp = Bd * Td, B * T
    M = Mp + Ms
    C = D - D_ATTN
    topk_p = min(IDX_TOPK_MAX, T // 4)
    topk_s = min(IDX_TOPK_MAX, (past + Td) // 4)
    l_s = past + Td
    s_pad = -(-l_s // KC) * KC
    tm_p, tm_s = 256, 128
    xp, xs = x_prompt, x_sample.reshape(1, Ms, D)
    outs = [[] for _ in range(8)]
    for l in range(depth):
        mods = _ada(jnp.concatenate([c_prompt, c_sample], axis=0), w_ada[l], b_ada[l])
        mp = [m[:, None, :] for m in jnp.split(mods[:B], 6, axis=-1)]
        ms = [jnp.repeat(m, Td, axis=0)[None] for m in jnp.split(mods[B:], 6, axis=-1)]
        w2, b2 = _repack_w_in(w_in[l], b_in[l])
        row = lambda a: a.reshape(1, -1)
        proj_w = (row(norm1_g[l]), w2, b2, row(q_norm_g[l]), row(k_norm_g[l]),
                  row(idx_k_ln_g[l]), row(idx_k_ln_b[l]))
        woa, wob = w_out[l, :D_ATTN].astype(BF16), w_out[l, D_ATTN:].astype(BF16)
        out_w = (row(norm2_g[l]), woa, wob, row(b_out[l]), w_router[l].astype(BF16),
                 w_sh_gate[l].astype(BF16), w_sh_up[l].astype(BF16), w_sh_down[l].astype(BF16))

        q, k, kb, v, vb, qi, ki, kib, wi, glu = _inproj(xp, mp[0], mp[1], *proj_w, tm=tm_p, per_token=False)
        attn = _dsa(q, qi, wi, kb, vb, kib, q_base=0, l_valid=T, topk=topk_p, causal=True)
        conv = _conv(glu, jnp.zeros((B, CTX_PAD, C), F32), dw_w[l], dw_b[l], conv_ln_g[l], conv_ln_b[l])
        base_p, h2_p, lg_p = _outproj(attn, conv, xp, mp[2], mp[3], mp[4], mp[5], *out_w,
                                      tm=tm_p, per_token=False)
        outs[0].append(k.reshape(B, T, N_HEADS, HEAD_DIM))
        outs[1].append(v.reshape(B, T, N_HEADS, HEAD_DIM))
        outs[2].append(ki)
        outs[3].append(glu[:, T - CTX:])

        q, k, kb, v, vb, qi, ki, kib, wi, glu = _inproj(xs, ms[0], ms[1], *proj_w, tm=tm_s, per_token=True)
        per_b = lambda a: a.reshape(Bd, Td, a.shape[-1])
        cat = lambda cache, new: _pad_rows(
            jnp.concatenate([cache.reshape(Bd, past, -1).astype(BF16), per_b(new)], axis=1), s_pad)
        attn = _dsa(_pad_rows(per_b(q), QB), _pad_rows(per_b(qi), QB), _pad_rows(per_b(wi), QB),
                    cat(cache_k[l], kb), cat(cache_v[l], vb), cat(cache_kidx[l], kib),
                    q_base=past, l_valid=l_s, topk=topk_s, causal=False)
        attn = attn[:, :Td].reshape(1, Ms, D_ATTN)
        ext = jnp.concatenate([state_conv[l], per_b(glu)], axis=1)
        ctx = jnp.pad(state_conv[l], ((0, 0), (CTX_PAD - CTX, 0), (0, 0)))
        conv = _conv(per_b(glu), ctx, dw_w[l], dw_b[l], conv_ln_g[l], conv_ln_b[l]).reshape(1, Ms, C)
        base_s, h2_s, lg_s = _outproj(attn, conv, xs, ms[2], ms[3], ms[4], ms[5], *out_w,
                                      tm=tm_s, per_token=True)
        outs[4].append(per_b(k).reshape(Bd, Td, N_HEADS, HEAD_DIM))
        outs[5].append(per_b(v).reshape(Bd, Td, N_HEADS, HEAD_DIM))
        outs[6].append(per_b(ki))
        outs[7].append(ext[:, Td:])

        h2 = jnp.concatenate([h2_p.reshape(Mp, D), h2_s.reshape(Ms, D)], axis=0)
        lg = jnp.concatenate([lg_p.reshape(Mp, -1), lg_s.reshape(Ms, -1)], axis=0)
        idx, gw = _route(lg, router_bias[l])
        y = _moe(h2, idx[:, :TOP_K], gw[:, :TOP_K], w_gate_e[l], w_up_e[l], w_down_e[l])
        xp = _combine(base_p, mp[5], y, m_total=M, tok0=0, tm=tm_p, per_token=False)
        xs = _combine(base_s, ms[5], y, m_total=M, tok0=Mp, tm=256, per_token=True)
    return (xp, xs.reshape(Bd, Td, D), *[jnp.stack(o) for o in outs])
```

```python
import functools

import numpy as np
import jax
import jax.numpy as jnp
from jax import lax
from jax.experimental import pallas as pl
from jax.experimental.pallas import tpu as pltpu

F32, BF16, I32, U32 = jnp.float32, jnp.bfloat16, jnp.int32, jnp.uint32

N_HEADS = 8
HEAD_DIM = 128
D_ATTN = N_HEADS * HEAD_DIM
IDX_HEADS = 16
IDX_DIM = 64
CHUNK = 64
CHUNK_SHIFT = 6
CONV_WIDTH = 31
CTX = CONV_WIDTH - 1
IDX_TOPK_MAX = 256
N_GROUPS = 8
TOPK_GROUPS = 4
TOP_K = 8
ROUTED_SCALE = 2.5
EPS = 1e-6
NEG_INF = -1e30

LANES = 128
SUBLANES = 8
VMEM_LIMIT = 56 * 1024 * 1024

QB = 128
KC = 256
CTX_PAD = 32
MOE_ROWS = 256
ROW_TILES = 16


def _cparams(sem):
    return pltpu.CompilerParams(dimension_semantics=sem, vmem_limit_bytes=VMEM_LIMIT)


def _const_spec(shape):
    nd = len(shape)
    return pl.BlockSpec(shape, lambda *_: (0,) * nd, pipeline_mode=pl.Buffered(1))


def _silu(x):
    return x * jax.nn.sigmoid(x)


def _pack_pairs(x):
    h = x.shape[-1] // 2
    bits = lax.bitcast_convert_type(x.astype(BF16).astype(F32), U32)
    return (bits[:, h:] & jnp.uint32(0xFFFF0000)) | (bits[:, :h] >> 16)


def _unpack_pairs(p):
    lo = lax.bitcast_convert_type(p << 16, F32)
    hi = lax.bitcast_convert_type(p & jnp.uint32(0xFFFF0000), F32)
    return lo, hi


def _tree_sum(parts):
    while len(parts) > 1:
        parts = [a + b for a, b in zip(parts[::2], parts[1::2])] + parts[len(parts) & ~1:]
    return parts[0]


def _ada_kernel(c_ref, w_ref, b_ref, o_ref):
    a = _silu(c_ref[...]).astype(BF16)
    o_ref[...] = jnp.dot(a, w_ref[...].astype(BF16), preferred_element_type=F32) + b_ref[...]


def _ada(c_all, w_ada, b_ada):
    R, D = c_all.shape
    N = w_ada.shape[1]
    TN = 1024
    return pl.pallas_call(
        _ada_kernel,
        grid=(N // TN,),
        in_specs=[pl.BlockSpec((R, D), lambda n: (0, 0)),
                  pl.BlockSpec((D, TN), lambda n: (0, n)),
                  pl.BlockSpec((1, TN), lambda n: (0, n))],
        out_specs=pl.BlockSpec((R, TN), lambda n: (0, n)),
        out_shape=jax.ShapeDtypeStruct((R, N), F32),
        compiler_params=_cparams(("arbitrary",)),
        name="ada",
    )(c_all, w_ada, b_ada.reshape(1, N))


C_Q, C_K, C_V, C_QI = 0, 1024, 2048, 3072
C_KI, C_WI, C_UV, C_UG, C_END = 4096, 4224, 4352, 5376, 6400


def _inproj_kernel(x_ref, sh_ref, sc_ref, g_ref, w_ref, b_ref, qg_ref, kg_ref, lg_ref, lb_ref,
                   q_ref, k_ref, kb_ref, v_ref, vb_ref, qi_ref, ki_ref, kib_ref, wi_ref, glu_ref):
    x = x_ref[...]
    y = x * lax.rsqrt(jnp.mean(x * x, axis=-1, keepdims=True) + EPS) * g_ref[...]
    hb = (y * (1.0 + sc_ref[...]) + sh_ref[...]).astype(BF16)

    def proj(lo, hi):
        return jnp.dot(hb, w_ref[:, lo:hi], preferred_element_type=F32) + b_ref[:, lo:hi]

    def head_norm(z, gain):
        return z * lax.rsqrt(jnp.mean(z * z, axis=-1, keepdims=True) + EPS) * gain

    zq = proj(C_Q, C_K)
    zk = proj(C_K, C_V)
    for h in range(N_HEADS):
        sl = slice(h * HEAD_DIM, (h + 1) * HEAD_DIM)
        q_ref[:, sl] = head_norm(zq[:, sl], qg_ref[...]).astype(BF16)
        kn = head_norm(zk[:, sl], kg_ref[...])
        k_ref[:, sl] = kn
        kb_ref[:, sl] = kn.astype(BF16)
    zv = proj(C_V, C_QI)
    v_ref[...] = zv
    vb_ref[...] = zv.astype(BF16)
    qi_ref[...] = (proj(C_QI, C_KI) * (IDX_DIM ** -0.5)).astype(BF16)
    zki = proj(C_KI, C_WI)[:, :IDX_DIM]
    mu = jnp.mean(zki, axis=-1, keepdims=True)
    xc = zki - mu
    var = jnp.mean(xc * xc, axis=-1, keepdims=True)
    ki = xc * lax.rsqrt(var + EPS) * lg_ref[...] + lb_ref[...]
    ki_ref[...] = ki
    kib_ref[...] = ki.astype(BF16)
    wi_ref[...] = proj(C_WI, C_UV)[:, :IDX_HEADS] * (IDX_HEADS ** -0.5)
    glu_ref[...] = proj(C_UV, C_UG) * jax.nn.sigmoid(proj(C_UG, C_END))


def _mod_spec(per_token, tm, d):
    if per_token:
        return pl.BlockSpec((None, tm, d), lambda b, t: (b, t, 0))
    return pl.BlockSpec((None, 1, d), lambda b, t: (b, 0, 0))


def _inproj(x, sh, sc, g, w2, b2, qg, kg, lg, lb, *, tm, per_token):
    Bx, T, D = x.shape
    tok = lambda n: pl.BlockSpec((None, tm, n), lambda b, t: (b, t, 0))
    shp = lambda n, dt: jax.ShapeDtypeStruct((Bx, T, n), dt)
    return pl.pallas_call(
        _inproj_kernel,
        grid=(Bx, T // tm),
        in_specs=[tok(D), _mod_spec(per_token, tm, D), _mod_spec(per_token, tm, D),
                  _const_spec((1, D)), _const_spec((D, C_END)), _const_spec((1, C_END)),
                  _const_spec((1, HEAD_DIM)), _const_spec((1, HEAD_DIM)),
                  _const_spec((1, IDX_DIM)), _const_spec((1, IDX_DIM))],
        out_specs=[tok(D_ATTN), tok(D_ATTN), tok(D_ATTN), tok(D_ATTN), tok(D_ATTN),
                   tok(IDX_HEADS * IDX_DIM), tok(IDX_DIM), tok(IDX_DIM), tok(IDX_HEADS),
                   tok(D - D_ATTN)],
        out_shape=[shp(D_ATTN, BF16), shp(D_ATTN, F32), shp(D_ATTN, BF16), shp(D_ATTN, F32),
                   shp(D_ATTN, BF16), shp(IDX_HEADS * IDX_DIM, BF16), shp(IDX_DIM, F32),
                   shp(IDX_DIM, BF16), shp(IDX_HEADS, F32), shp(D - D_ATTN, F32)],
        compiler_params=_cparams(("parallel", "arbitrary")),
        name="inproj",
    )(x, sh, sc, g, w2, b2, qg, kg, lg, lb)


def _alibi_slopes():
    return [float(np.float32(2.0) ** np.float32(-8.0 * h / N_HEADS)) for h in range(1, N_HEADS + 1)]


def _dsa_kernel(qT_ref, qiw_ref, wiw_ref, k_ref, vT_ref, ki_ref, o_ref,
                key_ref, sel_ref, acc_ref, m_ref, l_ref, *, q_base, l_valid, topk, causal, n_kc):
    jb = pl.program_id(1)
    nkc = (jb * QB + QB + KC - 1) // KC if causal else n_kc
    lane = lax.broadcasted_iota(I32, (1, QB), 1)
    qpos = q_base + jb * QB + lane
    qchunk = qpos >> CHUNK_SHIFT
    sub = lax.broadcasted_iota(I32, (KC, 1), 0)

    def visible(c):
        kpos = c * KC + sub
        return ((kpos >> CHUNK_SHIFT) <= qchunk) & (kpos < l_valid)

    def score_chunk(c, carry):
        kic = ki_ref[c]
        acc = jnp.zeros((KC, QB), F32)
        for hp in range(IDX_HEADS // 2):
            sl = slice(hp * 2 * QB, (hp + 1) * 2 * QB)
            r = jnp.dot(kic, qiw_ref[:, sl], preferred_element_type=F32)
            t = jnp.maximum(r, 0.0) * wiw_ref[:, sl]
            acc = acc + t[:, :QB] + t[:, QB:]
        bits = pltpu.bitcast(jnp.where(visible(c), acc, -jnp.inf), I32)
        key_ref[c] = bits ^ ((bits >> 31) & 0x7FFFFFFF)
        return carry

    lax.fori_loop(0, nkc, score_chunk, 0)

    def count(pred):
        def body(c, cnt):
            hit = jnp.where(pred(key_ref[c]), 1.0, 0.0)
            return cnt + _tree_sum([hit[i:i + SUBLANES] for i in range(0, KC, SUBLANES)])
        cnt = lax.fori_loop(0, nkc, body, jnp.zeros((SUBLANES, QB), F32))
        return jnp.sum(cnt, axis=0, keepdims=True)

    kf = float(topk)
    int_min = jnp.full((1, QB), -2 ** 31, I32)
    zero = jnp.zeros((1, QB), I32)
    thr0 = jnp.where(count(lambda key: key >= zero) >= kf, zero, int_min)

    def bisect(i, lo):
        cand = lo | jnp.left_shift(jnp.int32(1), 30 - i)
        return jnp.where(count(lambda key: key >= cand) >= kf, cand, lo)

    thr = lax.fori_loop(0, 31, bisect, thr0)
    need = kf - count(lambda key: key > thr)

    tri = (lax.broadcasted_iota(I32, (KC, KC), 0) >= lax.broadcasted_iota(I32, (KC, KC), 1))
    tri = jnp.where(tri, 1.0, 0.0).astype(BF16)

    def select_chunk(c, carry):
        key = key_ref[c]
        tie = key == thr
        tie_f = jnp.where(tie, 1.0, 0.0)
        pref = jnp.dot(tri, tie_f.astype(BF16), preferred_element_type=F32) + carry
        sel = ((key > thr) | (tie & (pref <= need))) & visible(c)
        sel_ref[c] = jnp.where(sel, 1.0, 0.0)
        return carry + jnp.sum(tie_f, axis=0, keepdims=True)

    lax.fori_loop(0, nkc, select_chunk, jnp.zeros((1, QB), F32))

    m_ref[...] = jnp.full((N_HEADS, QB), 0.1 * NEG_INF, F32)
    l_ref[...] = jnp.zeros((N_HEADS, QB), F32)
    acc_ref[...] = jnp.zeros((D_ATTN, QB), F32)
    scale = HEAD_DIM ** -0.5
    slopes = [s / scale for s in _alibi_slopes()]
    c2 = scale * float(np.log2(np.e))

    def attend_chunk(c, carry):
        dist = jnp.abs(qpos - (c * KC + sub)).astype(F32)
        selc = sel_ref[c] > 0.5
        for h in range(N_HEADS):
            hs = slice(h * HEAD_DIM, (h + 1) * HEAD_DIM)
            lt = jnp.dot(k_ref[c, :, hs], qT_ref[hs, :], preferred_element_type=F32)
            lt = jnp.where(selc, lt - slopes[h] * dist, NEG_INF)
            m_old = m_ref[h:h + 1, :]
            m_new = jnp.maximum(m_old, jnp.max(lt, axis=0, keepdims=True))
            alpha = jnp.exp2((m_old - m_new) * c2)
            p = jnp.exp2((lt - m_new) * c2)
            l_ref[h:h + 1, :] = alpha * l_ref[h:h + 1, :] + jnp.sum(p, axis=0, keepdims=True)
            pv = jnp.dot(vT_ref[c, hs, :], p.astype(BF16), preferred_element_type=F32)
            acc_ref[hs, :] = acc_ref[hs, :] * alpha + pv
            m_ref[h:h + 1, :] = m_new
        return carry

    lax.fori_loop(0, nkc, attend_chunk, 0)

    for h in range(N_HEADS):
        hs = slice(h * HEAD_DIM, (h + 1) * HEAD_DIM)
        o = acc_ref[hs, :] / l_ref[h:h + 1, :]
        o_ref[:, hs] = o.T.astype(BF16)


def _dsa(q, qi, wi, k, v, ki, *, q_base, l_valid, topk, causal):
    B, Tq, _ = q.shape
    S = k.shape[1]
    nj, n_kc = Tq // QB, S // KC
    qT = q.transpose(0, 2, 1)
    qiw = qi.reshape(B, nj, QB, IDX_HEADS, IDX_DIM).transpose(0, 1, 4, 3, 2)
    qiw = qiw.reshape(B, nj, IDX_DIM, IDX_HEADS * QB)
    wiw = wi.reshape(B, nj, QB, IDX_HEADS).transpose(0, 1, 3, 2).reshape(B, nj, 1, IDX_HEADS * QB)
    kc = k.reshape(B, n_kc, KC, D_ATTN)
    vT = v.reshape(B, n_kc, KC, D_ATTN).transpose(0, 1, 3, 2)
    kic = ki.reshape(B, n_kc, KC, IDX_DIM)
    kern = functools.partial(_dsa_kernel, q_base=q_base, l_valid=l_valid, topk=topk,
                             causal=causal, n_kc=n_kc)
    return pl.pallas_call(
        kern,
        grid=(B, nj),
        in_specs=[pl.BlockSpec((None, D_ATTN, QB), lambda b, j: (b, 0, j)),
                  pl.BlockSpec((None, None, IDX_DIM, IDX_HEADS * QB), lambda b, j: (b, j, 0, 0)),
                  pl.BlockSpec((None, None, 1, IDX_HEADS * QB), lambda b, j: (b, j, 0, 0)),
                  pl.BlockSpec((None, n_kc, KC, D_ATTN), lambda b, j: (b, 0, 0, 0)),
                  pl.BlockSpec((None, n_kc, D_ATTN, KC), lambda b, j: (b, 0, 0, 0)),
                  pl.BlockSpec((None, n_kc, KC, IDX_DIM), lambda b, j: (b, 0, 0, 0))],
        out_specs=pl.BlockSpec((None, QB, D_ATTN), lambda b, j: (b, j, 0)),
        out_shape=jax.ShapeDtypeStruct((B, Tq, D_ATTN), BF16),
        scratch_shapes=[pltpu.VMEM((n_kc, KC, QB), I32), pltpu.VMEM((n_kc, KC, QB), F32),
                        pltpu.VMEM((D_ATTN, QB), F32), pltpu.VMEM((N_HEADS, QB), F32),
                        pltpu.VMEM((N_HEADS, QB), F32)],
        compiler_params=_cparams(("parallel", "arbitrary")),
        name="dsa",
    )(qT, qiw, wiw, kc, vT, kic)


def _conv_kernel(glu_ref, ctx_ref, w_ref, b_ref, g_ref, be_ref, o_ref, ext_ref, *, T, rt):
    C = glu_ref.shape[-1]
    ext_ref[0:CTX_PAD, :] = ctx_ref[...]
    ext_ref[CTX_PAD:CTX_PAD + T, :] = glu_ref[...]
    ext_ref[CTX_PAD + T:, :] = jnp.zeros((SUBLANES, C), F32)
    first = CTX_PAD - CTX

    def tile(i, carry):
        t0 = pl.multiple_of(i * rt, rt)
        win = ext_ref[pl.ds(t0, rt + CTX_PAD + SUBLANES), :]
        nwin = rt + CTX_PAD + SUBLANES
        shifted = [win if r == 0 else pltpu.roll(win, nwin - r, axis=0) for r in range(SUBLANES)]
        acc = jnp.zeros((rt, C), F32) + b_ref[...]
        for kk in range(CONV_WIDTH):
            a, r = divmod(first + kk, SUBLANES)
            acc = acc + w_ref[kk:kk + 1, :] * shifted[r][SUBLANES * a:SUBLANES * a + rt]
        mu = jnp.mean(acc, axis=-1, keepdims=True)
        xc = acc - mu
        var = jnp.mean(xc * xc, axis=-1, keepdims=True)
        y = xc * lax.rsqrt(var + EPS) * g_ref[...] + be_ref[...]
        o_ref[pl.ds(t0, rt), :] = _silu(y).astype(BF16)
        return carry

    lax.fori_loop(0, T // rt, tile, 0)


def _conv(glu, ctx, dw_w, dw_b, ln_g, ln_b):
    B, T, C = glu.shape
    rt = min(64, T)
    wpad = jnp.pad(dw_w, ((0, CTX_PAD - CONV_WIDTH), (0, 0)))
    kern = functools.partial(_conv_kernel, T=T, rt=rt)
    return pl.pallas_call(
        kern,
        grid=(B,),
        in_specs=[pl.BlockSpec((None, T, C), lambda b: (b, 0, 0)),
                  pl.BlockSpec((None, CTX_PAD, C), lambda b: (b, 0, 0)),
                  _const_spec((CTX_PAD, C)), _const_spec((1, C)), _const_spec((1, C)),
                  _const_spec((1, C))],
        out_specs=pl.BlockSpec((None, T, C), lambda b: (b, 0, 0)),
        out_shape=jax.ShapeDtypeStruct((B, T, C), BF16),
        scratch_shapes=[pltpu.VMEM((T + CTX_PAD + SUBLANES, C), F32)],
        compiler_params=_cparams(("parallel",)),
        name="conv",
    )(glu, ctx, wpad, dw_b.reshape(1, C), ln_g.reshape(1, C), ln_b.reshape(1, C))


def _outproj_kernel(attn_ref, conv_ref, x_ref, g1_ref, sh2_ref, sc2_ref, g2_ref, n2_ref,
                    woa_ref, wob_ref, bo_ref, wr_ref, wsg_ref, wsu_ref, wsd_ref,
                    base_ref, h2_ref, lg_ref):
    mix = (jnp.dot(attn_ref[...], woa_ref[...], preferred_element_type=F32)
           + jnp.dot(conv_ref[...], wob_ref[...], preferred_element_type=F32) + bo_ref[...])
    x1 = x_ref[...] + g1_ref[...] * mix
    y = x1 * lax.rsqrt(jnp.mean(x1 * x1, axis=-1, keepdims=True) + EPS) * n2_ref[...]
    h2 = y * (1.0 + sc2_ref[...]) + sh2_ref[...]
    h2_ref[...] = _pack_pairs(h2)
    hb = h2.astype(BF16)
    lg_ref[...] = jnp.dot(hb, wr_ref[...], preferred_element_type=F32)
    sg = jnp.dot(hb, wsg_ref[...], preferred_element_type=F32)
    su = jnp.dot(hb, wsu_ref[...], preferred_element_type=F32)
    shared = jnp.dot((_silu(sg) * su).astype(BF16), wsd_ref[...], preferred_element_type=F32)
    base_ref[...] = x1 + g2_ref[...] * shared


def _outproj(attn, conv, x, g1, sh2, sc2, g2, n2, woa, wob, bo, wr, wsg, wsu, wsd, *, tm, per_token):
    Bx, T, D = x.shape
    E = wr.shape[1]
    Ds = wsg.shape[1]
    tok = lambda n: pl.BlockSpec((None, tm, n), lambda b, t: (b, t, 0))
    mod = lambda: _mod_spec(per_token, tm, D)
    return pl.pallas_call(
        _outproj_kernel,
        grid=(Bx, T // tm),
        in_specs=[tok(D_ATTN), tok(D - D_ATTN), tok(D), mod(), mod(), mod(), mod(),
                  _const_spec((1, D)), _const_spec((D_ATTN, D)), _const_spec((D - D_ATTN, D)),
                  _const_spec((1, D)), _const_spec((D, E)), _const_spec((D, Ds)),
                  _const_spec((D, Ds)), _const_spec((Ds, D))],
        out_specs=[tok(D), pl.BlockSpec((None, tm, None, D // 2), lambda b, t: (b, t, 0, 0)), tok(E)],
        out_shape=[jax.ShapeDtypeStruct((Bx, T, D), F32), jax.ShapeDtypeStruct((Bx, T, 1, D // 2), U32),
                   jax.ShapeDtypeStruct((Bx, T, E), F32)],
        compiler_params=_cparams(("parallel", "arbitrary")),
        name="outproj",
    )(attn, conv, x, g1, sh2, sc2, g2, n2, woa, wob, bo, wr, wsg, wsu, wsd)


def _route_kernel(lg_ref, bias_ref, idx_ref, w_ref):
    tm, E = lg_ref.shape
    gsz = E // N_GROUPS
    scores = jax.nn.sigmoid(lg_ref[...])
    biased = scores + bias_ref[...]
    lane = lax.broadcasted_iota(I32, (tm, E), 1).astype(F32)
    lane_i = lax.broadcasted_iota(I32, (tm, E), 1)
    in_group = lambda g: (lane_i >= g * gsz) & (lane_i < (g + 1) * gsz)
    ninf = -jnp.inf

    def first_argmax(vals):
        m = jnp.max(vals, axis=-1, keepdims=True)
        i = jnp.min(jnp.where(vals == m, lane, float(E)), axis=-1, keepdims=True)
        return m, i

    gs = []
    for g in range(N_GROUPS):
        mg = jnp.where(in_group(g), biased, ninf)
        m1, i1 = first_argmax(mg)
        m2 = jnp.max(jnp.where(lane == i1, ninf, mg), axis=-1, keepdims=True)
        gs.append(m1 + m2)
    keep = jnp.zeros((tm, E), jnp.bool_)
    for g in range(N_GROUPS):
        rank = jnp.zeros((tm, 1), F32)
        for o in range(N_GROUPS):
            if o == g:
                continue
            beats = (gs[o] > gs[g]) | ((gs[o] == gs[g]) & (o < g))
            rank = rank + jnp.where(beats, 1.0, 0.0)
        keep = keep | (in_group(g) & (rank < float(TOPK_GROUPS)))
    masked = jnp.where(keep, biased, ninf)
    out_lane = lax.broadcasted_iota(I32, (tm, LANES), 1)
    idx_out = jnp.zeros((tm, LANES), F32)
    w_out = jnp.zeros((tm, LANES), F32)
    for kk in range(TOP_K):
        _, i = first_argmax(masked)
        hit = lane == i
        wk = jnp.sum(jnp.where(hit, scores, 0.0), axis=-1, keepdims=True)
        masked = jnp.where(hit, ninf, masked)
        idx_out = jnp.where(out_lane == kk, i, idx_out)
        w_out = jnp.where(out_lane == kk, wk, w_out)
    w_out = w_out / jnp.sum(w_out, axis=-1, keepdims=True) * ROUTED_SCALE
    idx_ref[...] = idx_out.astype(I32)
    w_ref[...] = w_out


def _route(logits, bias):
    M, E = logits.shape
    tm = 256
    return pl.pallas_call(
        _route_kernel,
        grid=(M // tm,),
        in_specs=[pl.BlockSpec((tm, E), lambda i: (i, 0)), _const_spec((1, E))],
        out_specs=[pl.BlockSpec((tm, LANES), lambda i: (i, 0)), pl.BlockSpec((tm, LANES), lambda i: (i, 0))],
        out_shape=[jax.ShapeDtypeStruct((M, LANES), I32), jax.ShapeDtypeStruct((M, LANES), F32)],
        compiler_params=_cparams(("parallel",)),
        name="route",
    )(logits, bias.reshape(1, E))


def _moe_kernel(ce_ref, meta_ref, srcn_ref, srcc_ref, dst_ref, roww_ref, x_hbm, wg_ref, wu_ref, wd_ref,
                y_hbm, xbuf, ybuf, zbuf, wgb, wub, wdb, src_s, dst_s, gsem, ssem, isem):
    c = pl.program_id(0)
    n_used = meta_ref[0]
    n_steps = pl.num_programs(0)
    slot = c % 2
    R = MOE_ROWS

    def gather_copy(tok, s, i, u):
        return pltpu.make_async_copy(x_hbm.at[tok], xbuf.at[s, i, pl.ds(u, 1), :], gsem.at[s])

    def scatter_copy(row, s, i, u):
        return pltpu.make_async_copy(ybuf.at[s, i, pl.ds(u, 1), :], y_hbm.at[pl.ds(row, 1), :], ssem.at[s])

    def whole_buffer_wait(buf, sem, s):
        pltpu.make_async_copy(buf.at[1 - s], buf.at[s], sem.at[s]).wait()

    def src_copy(vmem_ref):
        return pltpu.make_async_copy(vmem_ref, src_s, isem.at[0])

    def dst_copy():
        return pltpu.make_async_copy(dst_ref, dst_s, isem.at[1])

    def for_rows(fn):
        def body(i, carry):
            for u in range(SUBLANES):
                fn(i, u)
            return carry
        lax.fori_loop(0, R // SUBLANES, body, 0)

    def issue_gather(s):
        for_rows(lambda i, u: gather_copy(src_s[0, 0, i * SUBLANES + u], s, i, u).start())

    @pl.when(c == 0)
    def _():
        cp = src_copy(srcc_ref)
        cp.start()
        cp.wait()
        issue_gather(0)

    src_copy(srcn_ref).start()
    dst_copy().start()

    @pl.when(c < n_used)
    def _():
        prev = jnp.maximum(c - 1, 0)

        @pl.when((c == 0) | (ce_ref[c] != ce_ref[prev]))
        def _():
            wgb[...] = wg_ref[...].astype(BF16)
            wub[...] = wu_ref[...].astype(BF16)
            wdb[...] = wd_ref[...].astype(BF16)

        whole_buffer_wait(xbuf, gsem, slot)

        @pl.when(c >= 2)
        def _():
            whole_buffer_wait(ybuf, ssem, slot)

    src_copy(srcn_ref).wait()
    dst_copy().wait()

    @pl.when(c + 1 < n_used)
    def _():
        issue_gather(1 - slot)

    @pl.when(c < n_used)
    def _():
        H = xbuf.shape[-1]
        lo, hi = _unpack_pairs(xbuf[slot].reshape(R, H))
        xb = jnp.concatenate([lo.astype(BF16), hi.astype(BF16)], axis=1)
        g = jnp.dot(xb, wgb[...], preferred_element_type=F32)
        u = jnp.dot(xb, wub[...], preferred_element_type=F32)
        y = jnp.dot((_silu(g) * u).astype(BF16), wdb[...], preferred_element_type=F32)
        ybuf[slot] = _pack_pairs(y * roww_ref[...]).reshape(R // SUBLANES, SUBLANES, H)
        for_rows(lambda i, u: scatter_copy(dst_s[0, 0, i * SUBLANES + u], slot, i, u).start())

    @pl.when(c == n_steps - 1)
    def _():
        last = n_used - 1
        whole_buffer_wait(ybuf, ssem, last % 2)

        @pl.when(n_used >= 2)
        def _():
            whole_buffer_wait(ybuf, ssem, (last - 1) % 2)

        zbuf[...] = jnp.zeros(zbuf.shape, U32)
        dump0 = y_hbm.shape[0] - 2 * R
        for s in range(2):
            cp = pltpu.make_async_copy(zbuf, y_hbm.at[pl.ds(dump0 + s * R, R), :], ssem.at[0])
            cp.start()
            cp.wait()


def _moe(h2p, idx, gw, w_gate_e, w_up_e, w_down_e):
    M, _, H = h2p.shape
    E, D, De = w_gate_e.shape
    assert D == 2 * H
    R = MOE_ROWS
    A = M * TOP_K
    nch = A // R + E
    flat_e = idx.reshape(A)
    order = jnp.argsort(flat_e, stable=True).astype(I32)
    experts = jnp.arange(E, dtype=I32)
    counts = jnp.sum((flat_e[None, :] == experts[:, None]).astype(I32), axis=1)
    start = jnp.cumsum(counts).astype(I32) - counts
    nch_e = (counts + R - 1) // R
    ch_end = jnp.cumsum(nch_e).astype(I32)
    ch_start = ch_end - nch_e
    n_used = ch_end[-1]
    cid = jnp.arange(nch, dtype=I32)
    ce = jnp.sum((ch_end[None, :] <= jnp.minimum(cid, n_used - 1)[:, None]).astype(I32), axis=1)
    ce = jnp.minimum(ce, E - 1)
    rr = jnp.arange(R, dtype=I32)[None, :]
    local = (cid - ch_start[ce])[:, None] * R + rr
    valid = (cid[:, None] < n_used) & (local < counts[ce][:, None])
    a = order[jnp.clip(start[ce][:, None] + local, 0, A - 1)]
    src = jnp.where(valid, a // TOP_K, 0)
    dst = jnp.where(valid, (a % TOP_K) * M + a // TOP_K, TOP_K * M + (cid[:, None] % 2) * R + rr)
    roww = jnp.where(valid, gw.reshape(A)[a], 0.0)
    meta = jnp.stack([n_used, n_used]).astype(I32)

    idx_spec = lambda f: pl.BlockSpec((1, 1, R), f)
    wspec = lambda a_, b_: pl.BlockSpec((None, a_, b_), lambda c, ce_r, m_r: (ce_r[c], 0, 0))
    return pl.pallas_call(
        _moe_kernel,
        grid_spec=pltpu.PrefetchScalarGridSpec(
            num_scalar_prefetch=2,
            grid=(nch,),
            in_specs=[idx_spec(lambda c, ce_r, m_r: (jnp.minimum(c + 1, nch - 1), 0, 0)),
                      idx_spec(lambda c, ce_r, m_r: (c, 0, 0)),
                      idx_spec(lambda c, ce_r, m_r: (c, 0, 0)),
                      pl.BlockSpec((None, R, 1), lambda c, ce_r, m_r: (c, 0, 0)),
                      pl.BlockSpec(memory_space=pl.ANY),
                      wspec(D, De), wspec(D, De), wspec(De, D)],
            out_specs=pl.BlockSpec(memory_space=pl.ANY),
            scratch_shapes=[pltpu.VMEM((2, R // SUBLANES, SUBLANES, H), U32),
                            pltpu.VMEM((2, R // SUBLANES, SUBLANES, H), U32), pltpu.VMEM((R, H), U32),
                            pltpu.VMEM((D, De), BF16), pltpu.VMEM((D, De), BF16),
                            pltpu.VMEM((De, D), BF16),
                            pltpu.SMEM((1, 1, R), I32), pltpu.SMEM((1, 1, R), I32),
                            pltpu.SemaphoreType.DMA((2,)), pltpu.SemaphoreType.DMA((2,)),
                            pltpu.SemaphoreType.DMA((2,))]),
        out_shape=jax.ShapeDtypeStruct((TOP_K * M + 2 * R, H), U32),
        compiler_params=_cparams(("arbitrary",)),
        name="moe",
    )(ce, meta, src.reshape(nch, 1, R), src.reshape(nch, 1, R), dst.reshape(nch, 1, R),
      roww.reshape(nch, R, 1), h2p, w_gate_e, w_up_e, w_down_e)


def _combine_kernel(base_ref, g2_ref, *refs):
    o_ref = refs[-1]
    H = o_ref.shape[-1] // 2
    for j in range(0, H, LANES):
        halves = [_unpack_pairs(r[:, j:j + LANES]) for r in refs[:-1]]
        for part, sl in ((0, slice(j, j + LANES)), (1, slice(H + j, H + j + LANES))):
            routed = _tree_sum([h[part] for h in halves])
            o_ref[:, sl] = base_ref[:, sl] + g2_ref[:, sl] * routed


def _combine(base, g2, y, *, m_total, tok0, tm, per_token):
    Bx, T, D = base.shape
    nt = T // tm

    def plane(kk):
        off = (kk * m_total + tok0) // tm
        return pl.BlockSpec((tm, D // 2), lambda b, t: (off + b * nt + t, 0))

    return pl.pallas_call(
        _combine_kernel,
        grid=(Bx, nt),
        in_specs=[pl.BlockSpec((None, tm, D), lambda b, t: (b, t, 0)), _mod_spec(per_token, tm, D)]
                 + [plane(kk) for kk in range(TOP_K)],
        out_specs=pl.BlockSpec((None, tm, D), lambda b, t: (b, t, 0)),
        out_shape=jax.ShapeDtypeStruct((Bx, T, D), F32),
        compiler_params=_cparams(("parallel", "arbitrary")),
        name="combine",
    )(base, g2, *([y] * TOP_K))


def _repack_w_in(w_in, b_in):
    c_ki = 3 * D_ATTN + IDX_HEADS * IDX_DIM
    c_wi = c_ki + IDX_DIM
    c_uv = c_wi + IDX_HEADS

    def pack(a):
        z = lambda n: jnp.zeros(a.shape[:-1] + (n,), a.dtype)
        return jnp.concatenate([a[..., :c_wi], z(C_WI - C_KI - IDX_DIM), a[..., c_wi:c_uv],
                                z(C_UV - C_WI - IDX_HEADS), a[..., c_uv:]], axis=-1)

    return pack(w_in).astype(BF16), pack(b_in[None, :])


def _pad_rows(a, n):
    return jnp.pad(a, ((0, 0), (0, n - a.shape[1]), (0, 0)))


def kernel(x_prompt, x_sample, cache_k, cache_v, cache_kidx, state_conv, c_prompt, c_sample, norm1_g, norm2_g, w_ada, b_ada, w_in, b_in, q_norm_g, k_norm_g, idx_k_ln_g, idx_k_ln_b, dw_w, dw_b, conv_ln_g, conv_ln_b, w_out, b_out, w_router, router_bias, w_gate_e, w_up_e, w_down_e, w_sh_gate, w_sh_up, w_sh_down):
    B, T, D = x_prompt.shape
    Bd, Td, _ = x_sample.shape
    depth = w_in.shape[0]
    past = cache_k.shape[2]
    Ms, Mp = Bd * Td, B * T
    M = Mp + Ms
    C = D - D_ATTN
    topk_p = min(IDX_TOPK_MAX, T // 4)
    topk_s = min(IDX_TOPK_MAX, (past + Td) // 4)
    l_s = past + Td
    s_pad = -(-l_s // KC) * KC
    tm_p, tm_s = 256, 128
    xp, xs = x_prompt, x_sample.reshape(1, Ms, D)
    outs = [[] for _ in range(8)]
    for l in range(depth):
        mods = _ada(jnp.concatenate([c_prompt, c_sample], axis=0), w_ada[l], b_ada[l])
        mp = [m[:, None, :] for m in jnp.split(mods[:B], 6, axis=-1)]
        ms = [jnp.repeat(m, Td, axis=0)[None] for m in jnp.split(mods[B:], 6, axis=-1)]
        w2, b2 = _repack_w_in(w_in[l], b_in[l])
        row = lambda a: a.reshape(1, -1)
        proj_w = (row(norm1_g[l]), w2, b2, row(q_norm_g[l]), row(k_norm_g[l]),
                  row(idx_k_ln_g[l]), row(idx_k_ln_b[l]))
        woa, wob = w_out[l, :D_ATTN].astype(BF16), w_out[l, D_ATTN:].astype(BF16)
        out_w = (row(norm2_g[l]), woa, wob, row(b_out[l]), w_router[l].astype(BF16),
                 w_sh_gate[l].astype(BF16), w_sh_up[l].astype(BF16), w_sh_down[l].astype(BF16))

        q, k, kb, v, vb, qi, ki, kib, wi, glu = _inproj(xp, mp[0], mp[1], *proj_w, tm=tm_p, per_token=False)
        attn = _dsa(q, qi, wi, kb, vb, kib, q_base=0, l_valid=T, topk=topk_p, causal=True)
        conv = _conv(glu, jnp.zeros((B, CTX_PAD, C), F32), dw_w[l], dw_b[l], conv_ln_g[l], conv_ln_b[l])
        base_p, h2_p, lg_p = _outproj(attn, conv, xp, mp[2], mp[3], mp[4], mp[5], *out_w,
                                      tm=tm_p, per_token=False)
        outs[0].append(k.reshape(B, T, N_HEADS, HEAD_DIM))
        outs[1].append(v.reshape(B, T, N_HEADS, HEAD_DIM))
        outs[2].append(ki)
        outs[3].append(glu[:, T - CTX:])

        q, k, kb, v, vb, qi, ki, kib, wi, glu = _inproj(xs, ms[0], ms[1], *proj_w, tm=tm_s, per_token=True)
        per_b = lambda a: a.reshape(Bd, Td, a.shape[-1])
        cat = lambda cache, new: _pad_rows(
            jnp.concatenate([cache.reshape(Bd, past, -1).astype(BF16), per_b(new)], axis=1), s_pad)
        attn = _dsa(_pad_rows(per_b(q), QB), _pad_rows(per_b(qi), QB), _pad_rows(per_b(wi), QB),
                    cat(cache_k[l], kb), cat(cache_v[l], vb), cat(cache_kidx[l], kib),
                    q_base=past, l_valid=l_s, topk=topk_s, causal=False)
        attn = attn[:, :Td].reshape(1, Ms, D_ATTN)
        ext = jnp.concatenate([state_conv[l], per_b(glu)], axis=1)
        ctx = jnp.pad(state_conv[l], ((0, 0), (CTX_PAD - CTX, 0), (0, 0)))
        conv = _conv(per_b(glu), ctx, dw_w[l], dw_b[l], conv_ln_g[l], conv_ln_b[l]).reshape(1, Ms, C)
        base_s, h2_s, lg_s = _outproj(attn, conv, xs, ms[2], ms[3], ms[4], ms[5], *out_w,
                                      tm=tm_s, per_token=True)
        outs[4].append(per_b(k).reshape(Bd, Td, N_HEADS, HEAD_DIM))
        outs[5].append(per_b(v).reshape(Bd, Td, N_HEADS, HEAD_DIM))
        outs[6].append(per_b(ki))
        outs[7].append(ext[:, Td:])

        h2 = jnp.concatenate([h2_p.reshape(Mp, 1, D // 2), h2_s.reshape(Ms, 1, D // 2)], axis=0)
        lg = jnp.concatenate([lg_p.reshape(Mp, -1), lg_s.reshape(Ms, -1)], axis=0)
        idx, gw = _route(lg, router_bias[l])
        y = _moe(h2, idx[:, :TOP_K], gw[:, :TOP_K], w_gate_e[l], w_up_e[l], w_down_e[l])
        xp = _combine(base_p, mp[5], y, m_total=M, tok0=0, tm=tm_p, per_token=False)
        xs = _combine(base_s, ms[5], y, m_total=M, tok0=Mp, tm=256, per_token=True)
    return (xp, xs.reshape(Bd, Td, D), *[jnp.stack(o) for o in outs])
```

```python
import functools

import numpy as np
import jax
import jax.numpy as jnp
from jax import lax
from jax.experimental import pallas as pl
from jax.experimental.pallas import tpu as pltpu

F32, BF16, I32, U32 = jnp.float32, jnp.bfloat16, jnp.int32, jnp.uint32

N_HEADS = 8
HEAD_DIM = 128
D_ATTN = N_HEADS * HEAD_DIM
IDX_HEADS = 16
IDX_DIM = 64
CHUNK = 64
CHUNK_SHIFT = 6
CONV_WIDTH = 31
CTX = CONV_WIDTH - 1
IDX_TOPK_MAX = 256
N_GROUPS = 8
TOPK_GROUPS = 4
TOP_K = 8
ROUTED_SCALE = 2.5
EPS = 1e-6
NEG_INF = -1e30

LANES = 128
SUBLANES = 8
VMEM_LIMIT = 56 * 1024 * 1024

QB = 128
KC = 256
CTX_PAD = 32
MOE_ROWS = 256
ROW_TILES = 16


def _cparams(sem):
    return pltpu.CompilerParams(dimension_semantics=sem, vmem_limit_bytes=VMEM_LIMIT)


def _const_spec(shape):
    nd = len(shape)
    return pl.BlockSpec(shape, lambda *_: (0,) * nd, pipeline_mode=pl.Buffered(1))


def _silu(x):
    return x * jax.nn.sigmoid(x)


def _pack_pairs(x):
    h = x.shape[-1] // 2
    bits = lax.bitcast_convert_type(x.astype(BF16).astype(F32), U32)
    return (bits[:, h:] & jnp.uint32(0xFFFF0000)) | (bits[:, :h] >> 16)


def _unpack_pairs(p):
    lo = lax.bitcast_convert_type(p << 16, F32)
    hi = lax.bitcast_convert_type(p & jnp.uint32(0xFFFF0000), F32)
    return lo, hi


def _tree_sum(parts):
    while len(parts) > 1:
        parts = [a + b for a, b in zip(parts[::2], parts[1::2])] + parts[len(parts) & ~1:]
    return parts[0]


def _ada_kernel(c_ref, w_ref, b_ref, o_ref):
    a = _silu(c_ref[...]).astype(BF16)
    o_ref[...] = jnp.dot(a, w_ref[...].astype(BF16), preferred_element_type=F32) + b_ref[...]


def _ada(c_all, w_ada, b_ada):
    R, D = c_all.shape
    N = w_ada.shape[1]
    TN = 1024
    return pl.pallas_call(
        _ada_kernel,
        grid=(N // TN,),
        in_specs=[pl.BlockSpec((R, D), lambda n: (0, 0)),
                  pl.BlockSpec((D, TN), lambda n: (0, n)),
                  pl.BlockSpec((1, TN), lambda n: (0, n))],
        out_specs=pl.BlockSpec((R, TN), lambda n: (0, n)),
        out_shape=jax.ShapeDtypeStruct((R, N), F32),
        compiler_params=_cparams(("arbitrary",)),
        name="ada",
    )(c_all, w_ada, b_ada.reshape(1, N))


C_Q, C_K, C_V, C_QI = 0, 1024, 2048, 3072
C_KI, C_WI, C_UV, C_UG, C_END = 4096, 4224, 4352, 5376, 6400


def _inproj_kernel(x_ref, sh_ref, sc_ref, g_ref, w_ref, b_ref, qg_ref, kg_ref, lg_ref, lb_ref,
                   q_ref, k_ref, kb_ref, v_ref, vb_ref, qi_ref, ki_ref, kib_ref, wi_ref, glu_ref):
    x = x_ref[...]
    y = x * lax.rsqrt(jnp.mean(x * x, axis=-1, keepdims=True) + EPS) * g_ref[...]
    hb = (y * (1.0 + sc_ref[...]) + sh_ref[...]).astype(BF16)

    def proj(lo, hi):
        return jnp.dot(hb, w_ref[:, lo:hi], preferred_element_type=F32) + b_ref[:, lo:hi]

    def head_norm(z, gain):
        return z * lax.rsqrt(jnp.mean(z * z, axis=-1, keepdims=True) + EPS) * gain

    zq = proj(C_Q, C_K)
    zk = proj(C_K, C_V)
    for h in range(N_HEADS):
        sl = slice(h * HEAD_DIM, (h + 1) * HEAD_DIM)
        q_ref[:, sl] = head_norm(zq[:, sl], qg_ref[...]).astype(BF16)
        kn = head_norm(zk[:, sl], kg_ref[...])
        k_ref[:, sl] = kn
        kb_ref[:, sl] = kn.astype(BF16)
    zv = proj(C_V, C_QI)
    v_ref[...] = zv
    vb_ref[...] = zv.astype(BF16)
    qi_ref[...] = (proj(C_QI, C_KI) * (IDX_DIM ** -0.5)).astype(BF16)
    zki = proj(C_KI, C_WI)[:, :IDX_DIM]
    mu = jnp.mean(zki, axis=-1, keepdims=True)
    xc = zki - mu
    var = jnp.mean(xc * xc, axis=-1, keepdims=True)
    ki = xc * lax.rsqrt(var + EPS) * lg_ref[...] + lb_ref[...]
    ki_ref[...] = ki
    kib_ref[...] = ki.astype(BF16)
    wi_ref[...] = proj(C_WI, C_UV)[:, :IDX_HEADS] * (IDX_HEADS ** -0.5)
    glu_ref[...] = proj(C_UV, C_UG) * jax.nn.sigmoid(proj(C_UG, C_END))


def _mod_spec(per_token, tm, d):
    if per_token:
        return pl.BlockSpec((None, tm, d), lambda b, t: (b, t, 0))
    return pl.BlockSpec((None, 1, d), lambda b, t: (b, 0, 0))


def _inproj(x, sh, sc, g, w2, b2, qg, kg, lg, lb, *, tm, per_token):
    Bx, T, D = x.shape
    tok = lambda n: pl.BlockSpec((None, tm, n), lambda b, t: (b, t, 0))
    shp = lambda n, dt: jax.ShapeDtypeStruct((Bx, T, n), dt)
    return pl.pallas_call(
        _inproj_kernel,
        grid=(Bx, T // tm),
        in_specs=[tok(D), _mod_spec(per_token, tm, D), _mod_spec(per_token, tm, D),
                  _const_spec((1, D)), _const_spec((D, C_END)), _const_spec((1, C_END)),
                  _const_spec((1, HEAD_DIM)), _const_spec((1, HEAD_DIM)),
                  _const_spec((1, IDX_DIM)), _const_spec((1, IDX_DIM))],
        out_specs=[tok(D_ATTN), tok(D_ATTN), tok(D_ATTN), tok(D_ATTN), tok(D_ATTN),
                   tok(IDX_HEADS * IDX_DIM), tok(IDX_DIM), tok(IDX_DIM), tok(IDX_HEADS),
                   tok(D - D_ATTN)],
        out_shape=[shp(D_ATTN, BF16), shp(D_ATTN, F32), shp(D_ATTN, BF16), shp(D_ATTN, F32),
                   shp(D_ATTN, BF16), shp(IDX_HEADS * IDX_DIM, BF16), shp(IDX_DIM, F32),
                   shp(IDX_DIM, BF16), shp(IDX_HEADS, F32), shp(D - D_ATTN, F32)],
        compiler_params=_cparams(("parallel", "arbitrary")),
        name="inproj",
    )(x, sh, sc, g, w2, b2, qg, kg, lg, lb)


def _alibi_slopes():
    return [float(np.float32(2.0) ** np.float32(-8.0 * h / N_HEADS)) for h in range(1, N_HEADS + 1)]


def _dsa_kernel(qT_ref, qiw_ref, wiw_ref, k_ref, vT_ref, ki_ref, o_ref,
                key_ref, sel_ref, acc_ref, m_ref, l_ref, *, q_base, l_valid, topk, causal, n_kc):
    jb = pl.program_id(1)
    nkc = (jb * QB + QB + KC - 1) // KC if causal else n_kc
    lane = lax.broadcasted_iota(I32, (1, QB), 1)
    qpos = q_base + jb * QB + lane
    qchunk = qpos >> CHUNK_SHIFT
    sub = lax.broadcasted_iota(I32, (KC, 1), 0)

    def visible(c):
        kpos = c * KC + sub
        return ((kpos >> CHUNK_SHIFT) <= qchunk) & (kpos < l_valid)

    def score_chunk(c, carry):
        kic = ki_ref[c]
        acc = jnp.zeros((KC, QB), F32)
        for hp in range(IDX_HEADS // 2):
            sl = slice(hp * 2 * QB, (hp + 1) * 2 * QB)
            r = jnp.dot(kic, qiw_ref[:, sl], preferred_element_type=F32)
            t = jnp.maximum(r, 0.0) * wiw_ref[:, sl]
            acc = acc + t[:, :QB] + t[:, QB:]
        bits = pltpu.bitcast(jnp.where(visible(c), acc, -jnp.inf), I32)
        key_ref[c] = bits ^ ((bits >> 31) & 0x7FFFFFFF)
        return carry

    lax.fori_loop(0, nkc, score_chunk, 0)

    def count(pred):
        def body(c, cnt):
            hit = jnp.where(pred(key_ref[c]), 1.0, 0.0)
            return cnt + _tree_sum([hit[i:i + SUBLANES] for i in range(0, KC, SUBLANES)])
        cnt = lax.fori_loop(0, nkc, body, jnp.zeros((SUBLANES, QB), F32))
        return jnp.sum(cnt, axis=0, keepdims=True)

    kf = float(topk)
    int_min = jnp.full((1, QB), -2 ** 31, I32)
    zero = jnp.zeros((1, QB), I32)
    thr0 = jnp.where(count(lambda key: key >= zero) >= kf, zero, int_min)

    def bisect(i, lo):
        cand = lo | jnp.left_shift(jnp.int32(1), 30 - i)
        return jnp.where(count(lambda key: key >= cand) >= kf, cand, lo)

    thr = lax.fori_loop(0, 31, bisect, thr0)
    need = kf - count(lambda key: key > thr)

    tri = (lax.broadcasted_iota(I32, (KC, KC), 0) >= lax.broadcasted_iota(I32, (KC, KC), 1))
    tri = jnp.where(tri, 1.0, 0.0).astype(BF16)

    def select_chunk(c, carry):
        key = key_ref[c]
        tie = key == thr
        tie_f = jnp.where(tie, 1.0, 0.0)
        pref = jnp.dot(tri, tie_f.astype(BF16), preferred_element_type=F32) + carry
        sel = ((key > thr) | (tie & (pref <= need))) & visible(c)
        sel_ref[c] = jnp.where(sel, 1.0, 0.0)
        return carry + jnp.sum(tie_f, axis=0, keepdims=True)

    lax.fori_loop(0, nkc, select_chunk, jnp.zeros((1, QB), F32))

    m_ref[...] = jnp.full((N_HEADS, QB), 0.1 * NEG_INF, F32)
    l_ref[...] = jnp.zeros((N_HEADS, QB), F32)
    acc_ref[...] = jnp.zeros((D_ATTN, QB), F32)
    scale = HEAD_DIM ** -0.5
    slopes = [s / scale for s in _alibi_slopes()]
    c2 = scale * float(np.log2(np.e))

    def attend_chunk(c, carry):
        dist = jnp.abs(qpos - (c * KC + sub)).astype(F32)
        selc = sel_ref[c] > 0.5
        for h in range(N_HEADS):
            hs = slice(h * HEAD_DIM, (h + 1) * HEAD_DIM)
            lt = jnp.dot(k_ref[c, :, hs], qT_ref[hs, :], preferred_element_type=F32)
            lt = jnp.where(selc, lt - slopes[h] * dist, NEG_INF)
            m_old = m_ref[h:h + 1, :]
            m_new = jnp.maximum(m_old, jnp.max(lt, axis=0, keepdims=True))
            alpha = jnp.exp2((m_old - m_new) * c2)
            p = jnp.exp2((lt - m_new) * c2)
            l_ref[h:h + 1, :] = alpha * l_ref[h:h + 1, :] + jnp.sum(p, axis=0, keepdims=True)
            pv = jnp.dot(vT_ref[c, hs, :], p.astype(BF16), preferred_element_type=F32)
            acc_ref[hs, :] = acc_ref[hs, :] * alpha + pv
            m_ref[h:h + 1, :] = m_new
        return carry

    lax.fori_loop(0, nkc, attend_chunk, 0)

    for h in range(N_HEADS):
        hs = slice(h * HEAD_DIM, (h + 1) * HEAD_DIM)
        o = acc_ref[hs, :] / l_ref[h:h + 1, :]
        o_ref[:, hs] = o.T.astype(BF16)


def _dsa(q, qi, wi, k, v, ki, *, q_base, l_valid, topk, causal):
    B, Tq, _ = q.shape
    S = k.shape[1]
    nj, n_kc = Tq // QB, S // KC
    qT = q.transpose(0, 2, 1)
    qiw = qi.reshape(B, nj, QB, IDX_HEADS, IDX_DIM).transpose(0, 1, 4, 3, 2)
    qiw = qiw.reshape(B, nj, IDX_DIM, IDX_HEADS * QB)
    wiw = wi.reshape(B, nj, QB, IDX_HEADS).transpose(0, 1, 3, 2).reshape(B, nj, 1, IDX_HEADS * QB)
    kc = k.reshape(B, n_kc, KC, D_ATTN)
    vT = v.reshape(B, n_kc, KC, D_ATTN).transpose(0, 1, 3, 2)
    kic = ki.reshape(B, n_kc, KC, IDX_DIM)
    kern = functools.partial(_dsa_kernel, q_base=q_base, l_valid=l_valid, topk=topk,
                             causal=causal, n_kc=n_kc)
    return pl.pallas_call(
        kern,
        grid=(B, nj),
        in_specs=[pl.BlockSpec((None, D_ATTN, QB), lambda b, j: (b, 0, j)),
                  pl.BlockSpec((None, None, IDX_DIM, IDX_HEADS * QB), lambda b, j: (b, j, 0, 0)),
                  pl.BlockSpec((None, None, 1, IDX_HEADS * QB), lambda b, j: (b, j, 0, 0)),
                  pl.BlockSpec((None, n_kc, KC, D_ATTN), lambda b, j: (b, 0, 0, 0)),
                  pl.BlockSpec((None, n_kc, D_ATTN, KC), lambda b, j: (b, 0, 0, 0)),
                  pl.BlockSpec((None, n_kc, KC, IDX_DIM), lambda b, j: (b, 0, 0, 0))],
        out_specs=pl.BlockSpec((None, QB, D_ATTN), lambda b, j: (b, j, 0)),
        out_shape=jax.ShapeDtypeStruct((B, Tq, D_ATTN), BF16),
        scratch_shapes=[pltpu.VMEM((n_kc, KC, QB), I32), pltpu.VMEM((n_kc, KC, QB), F32),
                        pltpu.VMEM((D_ATTN, QB), F32), pltpu.VMEM((N_HEADS, QB), F32),
                        pltpu.VMEM((N_HEADS, QB), F32)],
        compiler_params=_cparams(("parallel", "arbitrary")),
        name="dsa",
    )(qT, qiw, wiw, kc, vT, kic)


def _conv_kernel(glu_ref, ctx_ref, w_ref, b_ref, g_ref, be_ref, o_ref, ext_ref, *, T, rt):
    C = glu_ref.shape[-1]
    ext_ref[0:CTX_PAD, :] = ctx_ref[...]
    ext_ref[CTX_PAD:CTX_PAD + T, :] = glu_ref[...]
    ext_ref[CTX_PAD + T:, :] = jnp.zeros((SUBLANES, C), F32)
    first = CTX_PAD - CTX

    def tile(i, carry):
        t0 = pl.multiple_of(i * rt, rt)
        win = ext_ref[pl.ds(t0, rt + CTX_PAD + SUBLANES), :]
        nwin = rt + CTX_PAD + SUBLANES
        shifted = [win if r == 0 else pltpu.roll(win, nwin - r, axis=0) for r in range(SUBLANES)]
        acc = jnp.zeros((rt, C), F32) + b_ref[...]
        for kk in range(CONV_WIDTH):
            a, r = divmod(first + kk, SUBLANES)
            acc = acc + w_ref[kk:kk + 1, :] * shifted[r][SUBLANES * a:SUBLANES * a + rt]
        mu = jnp.mean(acc, axis=-1, keepdims=True)
        xc = acc - mu
        var = jnp.mean(xc * xc, axis=-1, keepdims=True)
        y = xc * lax.rsqrt(var + EPS) * g_ref[...] + be_ref[...]
        o_ref[pl.ds(t0, rt), :] = _silu(y).astype(BF16)
        return carry

    lax.fori_loop(0, T // rt, tile, 0)


def _conv(glu, ctx, dw_w, dw_b, ln_g, ln_b):
    B, T, C = glu.shape
    rt = min(64, T)
    wpad = jnp.pad(dw_w, ((0, CTX_PAD - CONV_WIDTH), (0, 0)))
    kern = functools.partial(_conv_kernel, T=T, rt=rt)
    return pl.pallas_call(
        kern,
        grid=(B,),
        in_specs=[pl.BlockSpec((None, T, C), lambda b: (b, 0, 0)),
                  pl.BlockSpec((None, CTX_PAD, C), lambda b: (b, 0, 0)),
                  _const_spec((CTX_PAD, C)), _const_spec((1, C)), _const_spec((1, C)),
                  _const_spec((1, C))],
        out_specs=pl.BlockSpec((None, T, C), lambda b: (b, 0, 0)),
        out_shape=jax.ShapeDtypeStruct((B, T, C), BF16),
        scratch_shapes=[pltpu.VMEM((T + CTX_PAD + SUBLANES, C), F32)],
        compiler_params=_cparams(("parallel",)),
        name="conv",
    )(glu, ctx, wpad, dw_b.reshape(1, C), ln_g.reshape(1, C), ln_b.reshape(1, C))


def _outproj_kernel(attn_ref, conv_ref, x_ref, g1_ref, sh2_ref, sc2_ref, g2_ref, n2_ref,
                    woa_ref, wob_ref, bo_ref, wr_ref, wsg_ref, wsu_ref, wsd_ref,
                    base_ref, h2_ref, lg_ref):
    mix = (jnp.dot(attn_ref[...], woa_ref[...], preferred_element_type=F32)
           + jnp.dot(conv_ref[...], wob_ref[...], preferred_element_type=F32) + bo_ref[...])
    x1 = x_ref[...] + g1_ref[...] * mix
    y = x1 * lax.rsqrt(jnp.mean(x1 * x1, axis=-1, keepdims=True) + EPS) * n2_ref[...]
    h2 = y * (1.0 + sc2_ref[...]) + sh2_ref[...]
    h2_ref[...] = _pack_pairs(h2)
    hb = h2.astype(BF16)
    lg_ref[...] = jnp.dot(hb, wr_ref[...], preferred_element_type=F32)
    sg = jnp.dot(hb, wsg_ref[...], preferred_element_type=F32)
    su = jnp.dot(hb, wsu_ref[...], preferred_element_type=F32)
    shared = jnp.dot((_silu(sg) * su).astype(BF16), wsd_ref[...], preferred_element_type=F32)
    base_ref[...] = x1 + g2_ref[...] * shared


def _outproj(attn, conv, x, g1, sh2, sc2, g2, n2, woa, wob, bo, wr, wsg, wsu, wsd, *, tm, per_token):
    Bx, T, D = x.shape
    E = wr.shape[1]
    Ds = wsg.shape[1]
    tok = lambda n: pl.BlockSpec((None, tm, n), lambda b, t: (b, t, 0))
    mod = lambda: _mod_spec(per_token, tm, D)
    return pl.pallas_call(
        _outproj_kernel,
        grid=(Bx, T // tm),
        in_specs=[tok(D_ATTN), tok(D - D_ATTN), tok(D), mod(), mod(), mod(), mod(),
                  _const_spec((1, D)), _const_spec((D_ATTN, D)), _const_spec((D - D_ATTN, D)),
                  _const_spec((1, D)), _const_spec((D, E)), _const_spec((D, Ds)),
                  _const_spec((D, Ds)), _const_spec((Ds, D))],
        out_specs=[tok(D), pl.BlockSpec((None, tm, None, D // 2), lambda b, t: (b, t, 0, 0)), tok(E)],
        out_shape=[jax.ShapeDtypeStruct((Bx, T, D), F32), jax.ShapeDtypeStruct((Bx, T, 1, D // 2), U32),
                   jax.ShapeDtypeStruct((Bx, T, E), F32)],
        compiler_params=_cparams(("parallel", "arbitrary")),
        name="outproj",
    )(attn, conv, x, g1, sh2, sc2, g2, n2, woa, wob, bo, wr, wsg, wsu, wsd)


def _route_kernel(lg_ref, bias_ref, idx_ref, w_ref):
    tm, E = lg_ref.shape
    gsz = E // N_GROUPS
    scores = jax.nn.sigmoid(lg_ref[...])
    biased = scores + bias_ref[...]
    lane = lax.broadcasted_iota(I32, (tm, E), 1).astype(F32)
    lane_i = lax.broadcasted_iota(I32, (tm, E), 1)
    in_group = lambda g: (lane_i >= g * gsz) & (lane_i < (g + 1) * gsz)
    ninf = -jnp.inf

    def first_argmax(vals):
        m = jnp.max(vals, axis=-1, keepdims=True)
        i = jnp.min(jnp.where(vals == m, lane, float(E)), axis=-1, keepdims=True)
        return m, i

    gs = []
    for g in range(N_GROUPS):
        mg = jnp.where(in_group(g), biased, ninf)
        m1, i1 = first_argmax(mg)
        m2 = jnp.max(jnp.where(lane == i1, ninf, mg), axis=-1, keepdims=True)
        gs.append(m1 + m2)
    keep = jnp.zeros((tm, E), jnp.bool_)
    for g in range(N_GROUPS):
        rank = jnp.zeros((tm, 1), F32)
        for o in range(N_GROUPS):
            if o == g:
                continue
            beats = (gs[o] > gs[g]) | ((gs[o] == gs[g]) & (o < g))
            rank = rank + jnp.where(beats, 1.0, 0.0)
        keep = keep | (in_group(g) & (rank < float(TOPK_GROUPS)))
    masked = jnp.where(keep, biased, ninf)
    out_lane = lax.broadcasted_iota(I32, (tm, LANES), 1)
    idx_out = jnp.zeros((tm, LANES), F32)
    w_out = jnp.zeros((tm, LANES), F32)
    for kk in range(TOP_K):
        _, i = first_argmax(masked)
        hit = lane == i
        wk = jnp.sum(jnp.where(hit, scores, 0.0), axis=-1, keepdims=True)
        masked = jnp.where(hit, ninf, masked)
        idx_out = jnp.where(out_lane == kk, i, idx_out)
        w_out = jnp.where(out_lane == kk, wk, w_out)
    w_out = w_out / jnp.sum(w_out, axis=-1, keepdims=True) * ROUTED_SCALE
    idx_ref[...] = idx_out.astype(I32)
    w_ref[...] = w_out


def _route(logits, bias):
    M, E = logits.shape
    tm = 256
    return pl.pallas_call(
        _route_kernel,
        grid=(M // tm,),
        in_specs=[pl.BlockSpec((tm, E), lambda i: (i, 0)), _const_spec((1, E))],
        out_specs=[pl.BlockSpec((tm, LANES), lambda i: (i, 0)), pl.BlockSpec((tm, LANES), lambda i: (i, 0))],
        out_shape=[jax.ShapeDtypeStruct((M, LANES), I32), jax.ShapeDtypeStruct((M, LANES), F32)],
        compiler_params=_cparams(("parallel",)),
        name="route",
    )(logits, bias.reshape(1, E))


def _concat_rows_kernel(a_hbm, b_hbm, o_hbm, sem):
    na, nb = a_hbm.shape[0], b_hbm.shape[0]
    copies = [pltpu.make_async_copy(a_hbm, o_hbm.at[pl.ds(0, na)], sem.at[0]),
              pltpu.make_async_copy(b_hbm, o_hbm.at[pl.ds(na, nb)], sem.at[1])]
    for cp in copies:
        cp.start()
    for cp in copies:
        cp.wait()


def _concat_rows(a, b):
    hbm = pl.BlockSpec(memory_space=pl.ANY)
    return pl.pallas_call(
        _concat_rows_kernel,
        in_specs=[hbm, hbm],
        out_specs=hbm,
        out_shape=jax.ShapeDtypeStruct((a.shape[0] + b.shape[0],) + a.shape[1:], a.dtype),
        scratch_shapes=[pltpu.SemaphoreType.DMA((2,))],
        name="concat_rows",
    )(a, b)


def _moe_kernel(ce_ref, nxt_ref, meta_ref, srcn_ref, srcc_ref, dst_ref, roww_ref, x_hbm, wg_hbm, wu_hbm,
                wd_hbm, y_hbm, xbuf, ybuf, zbuf, xbb, wgs, wus, wds, wgb, wub, wdb, src_s, dst_s,
                gsem, ssem, isem, wsem):
    c = pl.program_id(0)
    n_used = meta_ref[0]
    n_steps = pl.num_programs(0)
    slot = c % 2
    R = MOE_ROWS

    def gather_copy(tok, s, i, u):
        return pltpu.make_async_copy(x_hbm.at[tok], xbuf.at[s, i, pl.ds(u, 1), :], gsem.at[s])

    def scatter_copy(row, s, i, u):
        return pltpu.make_async_copy(ybuf.at[s, i, pl.ds(u, 1), :], y_hbm.at[pl.ds(row, 1), :], ssem.at[s])

    def whole_buffer_wait(buf, sem, s):
        pltpu.make_async_copy(buf.at[1 - s], buf.at[s], sem.at[s]).wait()

    def weight_copies(e):
        return [pltpu.make_async_copy(wg_hbm.at[e], wgs, wsem.at[0]),
                pltpu.make_async_copy(wu_hbm.at[e], wus, wsem.at[1]),
                pltpu.make_async_copy(wd_hbm.at[e], wds, wsem.at[2])]

    def src_copy(vmem_ref):
        return pltpu.make_async_copy(vmem_ref, src_s, isem.at[0])

    def dst_copy():
        return pltpu.make_async_copy(dst_ref, dst_s, isem.at[1])

    def for_rows(fn):
        def body(i, carry):
            for u in range(SUBLANES):
                fn(i, u)
            return carry
        lax.fori_loop(0, R // SUBLANES, body, 0)

    def issue_gather(s):
        for_rows(lambda i, u: gather_copy(src_s[0, 0, i * SUBLANES + u], s, i, u).start())

    @pl.when(c == 0)
    def _():
        for cp in weight_copies(ce_ref[0]):
            cp.start()
        cp = src_copy(srcc_ref)
        cp.start()
        cp.wait()
        issue_gather(0)

    src_copy(srcn_ref).start()
    dst_copy().start()

    @pl.when(c >= n_used)
    def _():
        src_copy(srcn_ref).wait()
        dst_copy().wait()

    @pl.when(c < n_used)
    def _():
        prev = jnp.maximum(c - 1, 0)

        @pl.when((c == 0) | (ce_ref[c] != ce_ref[prev]))
        def _():
            for cp in weight_copies(ce_ref[c]):
                cp.wait()
            for stage, dest in ((wgs, wgb), (wus, wub), (wds, wdb)):
                rows = stage.shape[0] // 8
                for r in range(0, stage.shape[0], rows):
                    dest[r:r + rows, :] = stage[r:r + rows, :].astype(BF16)

            @pl.when(nxt_ref[c] >= 0)
            def _():
                for cp in weight_copies(nxt_ref[c]):
                    cp.start()

        whole_buffer_wait(xbuf, gsem, slot)

        @pl.when(c >= 2)
        def _():
            whole_buffer_wait(ybuf, ssem, slot)

        H = xbuf.shape[-1]
        lo, hi = _unpack_pairs(xbuf[slot].reshape(R, H))
        xbb[:, :H] = lo.astype(BF16)
        xbb[:, H:] = hi.astype(BF16)

        src_copy(srcn_ref).wait()
        dst_copy().wait()

        @pl.when(c + 1 < n_used)
        def _():
            issue_gather(1 - slot)

        xb = xbb[...]
        g = jnp.dot(xb, wgb[...], preferred_element_type=F32)
        u = jnp.dot(xb, wub[...], preferred_element_type=F32)
        y = jnp.dot((_silu(g) * u).astype(BF16), wdb[...], preferred_element_type=F32)
        ybuf[slot] = _pack_pairs(y * roww_ref[...]).reshape(R // SUBLANES, SUBLANES, H)
        for_rows(lambda i, u_: scatter_copy(dst_s[0, 0, i * SUBLANES + u_], slot, i, u_).start())

    @pl.when(c == n_steps - 1)
    def _():
        last = n_used - 1
        whole_buffer_wait(ybuf, ssem, last % 2)

        @pl.when(n_used >= 2)
        def _():
            whole_buffer_wait(ybuf, ssem, (last - 1) % 2)

        zbuf[...] = jnp.zeros(zbuf.shape, U32)
        dump0 = y_hbm.shape[0] - 2 * R
        for s in range(2):
            cp = pltpu.make_async_copy(zbuf, y_hbm.at[pl.ds(dump0 + s * R, R), :], ssem.at[0])
            cp.start()
            cp.wait()


def _moe(h2p, idx, gw, w_gate_e, w_up_e, w_down_e):
    M, _, H = h2p.shape
    E, D, De = w_gate_e.shape
    assert D == 2 * H
    R = MOE_ROWS
    A = M * TOP_K
    nch = A // R + E
    flat_e = idx.reshape(A)
    order = jnp.argsort(flat_e, stable=True).astype(I32)
    experts = jnp.arange(E, dtype=I32)
    counts = jnp.sum((flat_e[None, :] == experts[:, None]).astype(I32), axis=1)
    start = jnp.cumsum(counts).astype(I32) - counts
    nch_e = (counts + R - 1) // R
    ch_end = jnp.cumsum(nch_e).astype(I32)
    ch_start = ch_end - nch_e
    n_used = ch_end[-1]
    cid = jnp.arange(nch, dtype=I32)
    ce = jnp.sum((ch_end[None, :] <= jnp.minimum(cid, n_used - 1)[:, None]).astype(I32), axis=1)
    ce = jnp.minimum(ce, E - 1)
    after = ch_end[ce]
    nxt = jnp.where(after < n_used, ce[jnp.minimum(after, nch - 1)], -1).astype(I32)
    rr = jnp.arange(R, dtype=I32)[None, :]
    local = (cid - ch_start[ce])[:, None] * R + rr
    valid = (cid[:, None] < n_used) & (local < counts[ce][:, None])
    a = order[jnp.clip(start[ce][:, None] + local, 0, A - 1)]
    src = jnp.where(valid, a // TOP_K, 0)
    dst = jnp.where(valid, (a % TOP_K) * M + a // TOP_K, TOP_K * M + (cid[:, None] % 2) * R + rr)
    roww = jnp.where(valid, gw.reshape(A)[a], 0.0)
    meta = jnp.stack([n_used, n_used]).astype(I32)

    idx_spec = lambda f: pl.BlockSpec((1, 1, R), f)
    hbm = pl.BlockSpec(memory_space=pl.ANY)
    return pl.pallas_call(
        _moe_kernel,
        grid_spec=pltpu.PrefetchScalarGridSpec(
            num_scalar_prefetch=3,
            grid=(nch,),
            in_specs=[idx_spec(lambda c, *_: (jnp.minimum(c + 1, nch - 1), 0, 0)),
                      idx_spec(lambda c, *_: (c, 0, 0)),
                      idx_spec(lambda c, *_: (c, 0, 0)),
                      pl.BlockSpec((None, R, 1), lambda c, *_: (c, 0, 0)),
                      hbm, hbm, hbm, hbm],
            out_specs=hbm,
            scratch_shapes=[pltpu.VMEM((2, R // SUBLANES, SUBLANES, H), U32),
                            pltpu.VMEM((2, R // SUBLANES, SUBLANES, H), U32), pltpu.VMEM((R, H), U32),
                            pltpu.VMEM((R, D), BF16), pltpu.VMEM((D, De), F32), pltpu.VMEM((D, De), F32), pltpu.VMEM((De, D), F32),
                            pltpu.VMEM((D, De), BF16), pltpu.VMEM((D, De), BF16),
                            pltpu.VMEM((De, D), BF16),
                            pltpu.SMEM((1, 1, R), I32), pltpu.SMEM((1, 1, R), I32),
                            pltpu.SemaphoreType.DMA((2,)), pltpu.SemaphoreType.DMA((2,)),
                            pltpu.SemaphoreType.DMA((2,)), pltpu.SemaphoreType.DMA((3,))]),
        out_shape=jax.ShapeDtypeStruct((TOP_K * M + 2 * R, H), U32),
        compiler_params=_cparams(("arbitrary",)),
        name="moe",
    )(ce, nxt, meta, src.reshape(nch, 1, R), src.reshape(nch, 1, R), dst.reshape(nch, 1, R),
      roww.reshape(nch, R, 1), h2p, w_gate_e, w_up_e, w_down_e)


def _combine_kernel(base_ref, g2_ref, *refs):
    o_ref = refs[-1]
    H = o_ref.shape[-1] // 2
    for j in range(0, H, LANES):
        halves = [_unpack_pairs(r[:, j:j + LANES]) for r in refs[:-1]]
        for part, sl in ((0, slice(j, j + LANES)), (1, slice(H + j, H + j + LANES))):
            routed = _tree_sum([h[part] for h in halves])
            o_ref[:, sl] = base_ref[:, sl] + g2_ref[:, sl] * routed


def _combine(base, g2, y, *, m_total, tok0, tm, per_token):
    Bx, T, D = base.shape
    nt = T // tm

    def plane(kk):
        off = (kk * m_total + tok0) // tm
        return pl.BlockSpec((tm, D // 2), lambda b, t: (off + b * nt + t, 0))

    return pl.pallas_call(
        _combine_kernel,
        grid=(Bx, nt),
        in_specs=[pl.BlockSpec((None, tm, D), lambda b, t: (b, t, 0)), _mod_spec(per_token, tm, D)]
                 + [plane(kk) for kk in range(TOP_K)],
        out_specs=pl.BlockSpec((None, tm, D), lambda b, t: (b, t, 0)),
        out_shape=jax.ShapeDtypeStruct((Bx, T, D), F32),
        compiler_params=_cparams(("parallel", "arbitrary")),
        name="combine",
    )(base, g2, *([y] * TOP_K))


def _repack_w_in(w_in, b_in):
    c_ki = 3 * D_ATTN + IDX_HEADS * IDX_DIM
    c_wi = c_ki + IDX_DIM
    c_uv = c_wi + IDX_HEADS

    def pack(a):
        z = lambda n: jnp.zeros(a.shape[:-1] + (n,), a.dtype)
        return jnp.concatenate([a[..., :c_wi], z(C_WI - C_KI - IDX_DIM), a[..., c_wi:c_uv],
                                z(C_UV - C_WI - IDX_HEADS), a[..., c_uv:]], axis=-1)

    return pack(w_in).astype(BF16), pack(b_in[None, :])


def _pad_rows(a, n):
    return jnp.pad(a, ((0, 0), (0, n - a.shape[1]), (0, 0)))


def kernel(x_prompt, x_sample, cache_k, cache_v, cache_kidx, state_conv, c_prompt, c_sample, norm1_g, norm2_g, w_ada, b_ada, w_in, b_in, q_norm_g, k_norm_g, idx_k_ln_g, idx_k_ln_b, dw_w, dw_b, conv_ln_g, conv_ln_b, w_out, b_out, w_router, router_bias, w_gate_e, w_up_e, w_down_e, w_sh_gate, w_sh_up, w_sh_down):
    B, T, D = x_prompt.shape
    Bd, Td, _ = x_sample.shape
    depth = w_in.shape[0]
    past = cache_k.shape[2]
    Ms, Mp = Bd * Td, B * T
    M = Mp + Ms
    C = D - D_ATTN
    topk_p = min(IDX_TOPK_MAX, T // 4)
    topk_s = min(IDX_TOPK_MAX, (past + Td) // 4)
    l_s = past + Td
    s_pad = -(-l_s // KC) * KC
    tm_p, tm_s = 256, 128
    xp, xs = x_prompt, x_sample.reshape(1, Ms, D)
    outs = [[] for _ in range(8)]
    for l in range(depth):
        mods = _ada(jnp.concatenate([c_prompt, c_sample], axis=0), w_ada[l], b_ada[l])
        mp = [m[:, None, :] for m in jnp.split(mods[:B], 6, axis=-1)]
        ms = [jnp.repeat(m, Td, axis=0)[None] for m in jnp.split(mods[B:], 6, axis=-1)]
        w2, b2 = _repack_w_in(w_in[l], b_in[l])
        row = lambda a: a.reshape(1, -1)
        proj_w = (row(norm1_g[l]), w2, b2, row(q_norm_g[l]), row(k_norm_g[l]),
                  row(idx_k_ln_g[l]), row(idx_k_ln_b[l]))
        woa, wob = w_out[l, :D_ATTN].astype(BF16), w_out[l, D_ATTN:].astype(BF16)
        out_w = (row(norm2_g[l]), woa, wob, row(b_out[l]), w_router[l].astype(BF16),
                 w_sh_gate[l].astype(BF16), w_sh_up[l].astype(BF16), w_sh_down[l].astype(BF16))

        q, k, kb, v, vb, qi, ki, kib, wi, glu = _inproj(xp, mp[0], mp[1], *proj_w, tm=tm_p, per_token=False)
        attn = _dsa(q, qi, wi, kb, vb, kib, q_base=0, l_valid=T, topk=topk_p, causal=True)
        conv = _conv(glu, jnp.zeros((B, CTX_PAD, C), F32), dw_w[l], dw_b[l], conv_ln_g[l], conv_ln_b[l])
        base_p, h2_p, lg_p = _outproj(attn, conv, xp, mp[2], mp[3], mp[4], mp[5], *out_w,
                                      tm=tm_p, per_token=False)
        outs[0].append(k.reshape(B, T, N_HEADS, HEAD_DIM))
        outs[1].append(v.reshape(B, T, N_HEADS, HEAD_DIM))
        outs[2].append(ki)
        outs[3].append(glu[:, T - CTX:])

        q, k, kb, v, vb, qi, ki, kib, wi, glu = _inproj(xs, ms[0], ms[1], *proj_w, tm=tm_s, per_token=True)
        per_b = lambda a: a.reshape(Bd, Td, a.shape[-1])
        cat = lambda cache, new: _pad_rows(
            jnp.concatenate([cache.reshape(Bd, past, -1).astype(BF16), per_b(new)], axis=1), s_pad)
        attn = _dsa(_pad_rows(per_b(q), QB), _pad_rows(per_b(qi), QB), _pad_rows(per_b(wi), QB),
                    cat(cache_k[l], kb), cat(cache_v[l], vb), cat(cache_kidx[l], kib),
                    q_base=past, l_valid=l_s, topk=topk_s, causal=False)
        attn = attn[:, :Td].reshape(1, Ms, D_ATTN)
        ext = jnp.concatenate([state_conv[l], per_b(glu)], axis=1)
        ctx = jnp.pad(state_conv[l], ((0, 0), (CTX_PAD - CTX, 0), (0, 0)))
        conv = _conv(per_b(glu), ctx, dw_w[l], dw_b[l], conv_ln_g[l], conv_ln_b[l]).reshape(1, Ms, C)
        base_s, h2_s, lg_s = _outproj(attn, conv, xs, ms[2], ms[3], ms[4], ms[5], *out_w,
                                      tm=tm_s, per_token=True)
        outs[4].append(per_b(k).reshape(Bd, Td, N_HEADS, HEAD_DIM))
        outs[5].append(per_b(v).reshape(Bd, Td, N_HEADS, HEAD_DIM))
        outs[6].append(per_b(ki))
        outs[7].append(ext[:, Td:])

        h2 = _concat_rows(h2_p.reshape(Mp, 1, D // 2), h2_s.reshape(Ms, 1, D // 2))
        lg = jnp.concatenate([lg_p.reshape(Mp, -1), lg_s.reshape(Ms, -1)], axis=0)
        idx, gw = _route(lg, router_bias[l])
        y = _moe(h2, idx[:, :TOP_K], gw[:, :TOP_K], w_gate_e[l], w_up_e[l], w_down_e[l])
        xp = _combine(base_p, mp[5], y, m_total=M, tok0=0, tm=tm_p, per_token=False)
        xs = _combine(base_s, ms[5], y, m_total=M, tok0=Mp, tm=256, per_token=True)
    return (xp, xs.reshape(Bd, Td, D), *[jnp.stack(o) for o in outs])
```

```python
import functools

import numpy as np
import jax
import jax.numpy as jnp
from jax import lax
from jax.experimental import pallas as pl
from jax.experimental.pallas import tpu as pltpu

F32, BF16, I32, U32 = jnp.float32, jnp.bfloat16, jnp.int32, jnp.uint32

N_HEADS = 8
HEAD_DIM = 128
D_ATTN = N_HEADS * HEAD_DIM
IDX_HEADS = 16
IDX_DIM = 64
CHUNK = 64
CHUNK_SHIFT = 6
CONV_WIDTH = 31
CTX = CONV_WIDTH - 1
IDX_TOPK_MAX = 256
N_GROUPS = 8
TOPK_GROUPS = 4
TOP_K = 8
ROUTED_SCALE = 2.5
EPS = 1e-6
NEG_INF = -1e30

LANES = 128
SUBLANES = 8
VMEM_LIMIT = 56 * 1024 * 1024

QB = 128
KC = 256
CTX_PAD = 32
MOE_ROWS = 256
ROW_TILES = 16


def _cparams(sem):
    return pltpu.CompilerParams(dimension_semantics=sem, vmem_limit_bytes=VMEM_LIMIT)


def _const_spec(shape):
    nd = len(shape)
    return pl.BlockSpec(shape, lambda *_: (0,) * nd, pipeline_mode=pl.Buffered(1))


def _silu(x):
    return x * jax.nn.sigmoid(x)


def _pack_pairs(x):
    h = x.shape[-1] // 2
    bits = lax.bitcast_convert_type(x.astype(BF16).astype(F32), U32)
    return (bits[:, h:] & jnp.uint32(0xFFFF0000)) | (bits[:, :h] >> 16)


def _unpack_pairs(p):
    lo = lax.bitcast_convert_type(p << 16, F32)
    hi = lax.bitcast_convert_type(p & jnp.uint32(0xFFFF0000), F32)
    return lo, hi


def _tree_sum(parts):
    while len(parts) > 1:
        parts = [a + b for a, b in zip(parts[::2], parts[1::2])] + parts[len(parts) & ~1:]
    return parts[0]


def _ada_kernel(c_ref, w_ref, b_ref, o_ref):
    a = _silu(c_ref[...]).astype(BF16)
    o_ref[...] = jnp.dot(a, w_ref[...].astype(BF16), preferred_element_type=F32) + b_ref[...]


def _ada(c_all, w_ada, b_ada):
    R, D = c_all.shape
    N = w_ada.shape[1]
    TN = 1024
    return pl.pallas_call(
        _ada_kernel,
        grid=(N // TN,),
        in_specs=[pl.BlockSpec((R, D), lambda n: (0, 0)),
                  pl.BlockSpec((D, TN), lambda n: (0, n)),
                  pl.BlockSpec((1, TN), lambda n: (0, n))],
        out_specs=pl.BlockSpec((R, TN), lambda n: (0, n)),
        out_shape=jax.ShapeDtypeStruct((R, N), F32),
        compiler_params=_cparams(("arbitrary",)),
        name="ada",
    )(c_all, w_ada, b_ada.reshape(1, N))


C_Q, C_K, C_V, C_QI = 0, 1024, 2048, 3072
C_KI, C_WI, C_UV, C_UG, C_END = 4096, 4224, 4352, 5376, 6400


def _inproj_kernel(x_ref, sh_ref, sc_ref, g_ref, w_ref, b_ref, qg_ref, kg_ref, lg_ref, lb_ref,
                   q_ref, k_ref, kb_ref, v_ref, vb_ref, qi_ref, ki_ref, kib_ref, wi_ref, glu_ref):
    x = x_ref[...]
    y = x * lax.rsqrt(jnp.mean(x * x, axis=-1, keepdims=True) + EPS) * g_ref[...]
    hb = (y * (1.0 + sc_ref[...]) + sh_ref[...]).astype(BF16)

    def proj(lo, hi):
        return jnp.dot(hb, w_ref[:, lo:hi], preferred_element_type=F32) + b_ref[:, lo:hi]

    def head_norm(z, gain):
        return z * lax.rsqrt(jnp.mean(z * z, axis=-1, keepdims=True) + EPS) * gain

    zq = proj(C_Q, C_K)
    zk = proj(C_K, C_V)
    for h in range(N_HEADS):
        sl = slice(h * HEAD_DIM, (h + 1) * HEAD_DIM)
        q_ref[:, sl] = head_norm(zq[:, sl], qg_ref[...]).astype(BF16)
        kn = head_norm(zk[:, sl], kg_ref[...])
        k_ref[:, sl] = kn
        kb_ref[:, sl] = kn.astype(BF16)
    zv = proj(C_V, C_QI)
    v_ref[...] = zv
    vb_ref[...] = zv.astype(BF16)
    qi_ref[...] = (proj(C_QI, C_KI) * (IDX_DIM ** -0.5)).astype(BF16)
    zki = proj(C_KI, C_WI)[:, :IDX_DIM]
    mu = jnp.mean(zki, axis=-1, keepdims=True)
    xc = zki - mu
    var = jnp.mean(xc * xc, axis=-1, keepdims=True)
    ki = xc * lax.rsqrt(var + EPS) * lg_ref[...] + lb_ref[...]
    ki_ref[...] = ki
    kib_ref[...] = ki.astype(BF16)
    wi_ref[...] = proj(C_WI, C_UV)[:, :IDX_HEADS] * (IDX_HEADS ** -0.5)
    glu_ref[...] = proj(C_UV, C_UG) * jax.nn.sigmoid(proj(C_UG, C_END))


def _mod_spec(per_token, tm, d):
    if per_token:
        return pl.BlockSpec((None, tm, d), lambda b, t: (b, t, 0))
    return pl.BlockSpec((None, 1, d), lambda b, t: (b, 0, 0))


def _inproj(x, sh, sc, g, w2, b2, qg, kg, lg, lb, *, tm, per_token):
    Bx, T, D = x.shape
    tok = lambda n: pl.BlockSpec((None, tm, n), lambda b, t: (b, t, 0))
    shp = lambda n, dt: jax.ShapeDtypeStruct((Bx, T, n), dt)
    return pl.pallas_call(
        _inproj_kernel,
        grid=(Bx, T // tm),
        in_specs=[tok(D), _mod_spec(per_token, tm, D), _mod_spec(per_token, tm, D),
                  _const_spec((1, D)), _const_spec((D, C_END)), _const_spec((1, C_END)),
                  _const_spec((1, HEAD_DIM)), _const_spec((1, HEAD_DIM)),
                  _const_spec((1, IDX_DIM)), _const_spec((1, IDX_DIM))],
        out_specs=[tok(D_ATTN), tok(D_ATTN), tok(D_ATTN), tok(D_ATTN), tok(D_ATTN),
                   tok(IDX_HEADS * IDX_DIM), tok(IDX_DIM), tok(IDX_DIM), tok(IDX_HEADS),
                   tok(D - D_ATTN)],
        out_shape=[shp(D_ATTN, BF16), shp(D_ATTN, F32), shp(D_ATTN, BF16), shp(D_ATTN, F32),
                   shp(D_ATTN, BF16), shp(IDX_HEADS * IDX_DIM, BF16), shp(IDX_DIM, F32),
                   shp(IDX_DIM, BF16), shp(IDX_HEADS, F32), shp(D - D_ATTN, F32)],
        compiler_params=_cparams(("parallel", "arbitrary")),
        name="inproj",
    )(x, sh, sc, g, w2, b2, qg, kg, lg, lb)


def _alibi_slopes():
    return [float(np.float32(2.0) ** np.float32(-8.0 * h / N_HEADS)) for h in range(1, N_HEADS + 1)]


def _dsa_kernel(qT_ref, qiw_ref, wiw_ref, k_ref, vT_ref, ki_ref, o_ref,
                key_ref, sel_ref, acc_ref, m_ref, l_ref, *, q_base, l_valid, topk, causal, n_kc):
    jb = pl.program_id(1)
    nkc = (jb * QB + QB + KC - 1) // KC if causal else n_kc
    lane = lax.broadcasted_iota(I32, (1, QB), 1)
    qpos = q_base + jb * QB + lane
    qchunk = qpos >> CHUNK_SHIFT
    sub = lax.broadcasted_iota(I32, (KC, 1), 0)

    def visible(c):
        kpos = c * KC + sub
        return ((kpos >> CHUNK_SHIFT) <= qchunk) & (kpos < l_valid)

    def score_chunk(c, carry):
        kic = ki_ref[c]
        acc = jnp.zeros((KC, QB), F32)
        for hp in range(IDX_HEADS // 2):
            sl = slice(hp * 2 * QB, (hp + 1) * 2 * QB)
            r = jnp.dot(kic, qiw_ref[:, sl], preferred_element_type=F32)
            t = jnp.maximum(r, 0.0) * wiw_ref[:, sl]
            acc = acc + t[:, :QB] + t[:, QB:]
        bits = pltpu.bitcast(jnp.where(visible(c), acc, -jnp.inf), I32)
        key_ref[c] = bits ^ ((bits >> 31) & 0x7FFFFFFF)
        return carry

    lax.fori_loop(0, nkc, score_chunk, 0)

    def count(pred):
        def body(c, cnt):
            hit = jnp.where(pred(key_ref[c]), 1.0, 0.0)
            return cnt + _tree_sum([hit[i:i + SUBLANES] for i in range(0, KC, SUBLANES)])
        cnt = lax.fori_loop(0, nkc, body, jnp.zeros((SUBLANES, QB), F32))
        return jnp.sum(cnt, axis=0, keepdims=True)

    kf = float(topk)
    int_min = jnp.full((1, QB), -2 ** 31, I32)
    zero = jnp.zeros((1, QB), I32)
    thr0 = jnp.where(count(lambda key: key >= zero) >= kf, zero, int_min)

    def bisect(i, lo):
        cand = lo | jnp.left_shift(jnp.int32(1), 30 - i)
        return jnp.where(count(lambda key: key >= cand) >= kf, cand, lo)

    thr = lax.fori_loop(0, 31, bisect, thr0)
    need = kf - count(lambda key: key > thr)

    tri = (lax.broadcasted_iota(I32, (KC, KC), 0) >= lax.broadcasted_iota(I32, (KC, KC), 1))
    tri = jnp.where(tri, 1.0, 0.0).astype(BF16)

    def select_chunk(c, carry):
        key = key_ref[c]
        tie = key == thr
        tie_f = jnp.where(tie, 1.0, 0.0)
        pref = jnp.dot(tri, tie_f.astype(BF16), preferred_element_type=F32) + carry
        sel = ((key > thr) | (tie & (pref <= need))) & visible(c)
        sel_ref[c] = jnp.where(sel, 1.0, 0.0)
        return carry + jnp.sum(tie_f, axis=0, keepdims=True)

    lax.fori_loop(0, nkc, select_chunk, jnp.zeros((1, QB), F32))

    m_ref[...] = jnp.full((N_HEADS, QB), 0.1 * NEG_INF, F32)
    l_ref[...] = jnp.zeros((N_HEADS, QB), F32)
    acc_ref[...] = jnp.zeros((D_ATTN, QB), F32)
    scale = HEAD_DIM ** -0.5
    slopes = [s / scale for s in _alibi_slopes()]
    c2 = scale * float(np.log2(np.e))

    def attend_chunk(c, carry):
        dist = jnp.abs(qpos - (c * KC + sub)).astype(F32)
        selc = sel_ref[c] > 0.5
        for h in range(N_HEADS):
            hs = slice(h * HEAD_DIM, (h + 1) * HEAD_DIM)
            lt = jnp.dot(k_ref[c, :, hs], qT_ref[hs, :], preferred_element_type=F32)
            lt = jnp.where(selc, lt - slopes[h] * dist, NEG_INF)
            m_old = m_ref[h:h + 1, :]
            m_new = jnp.maximum(m_old, jnp.max(lt, axis=0, keepdims=True))
            alpha = jnp.exp2((m_old - m_new) * c2)
            p = jnp.exp2((lt - m_new) * c2)
            l_ref[h:h + 1, :] = alpha * l_ref[h:h + 1, :] + jnp.sum(p, axis=0, keepdims=True)
            pv = jnp.dot(vT_ref[c, hs, :], p.astype(BF16), preferred_element_type=F32)
            acc_ref[hs, :] = acc_ref[hs, :] * alpha + pv
            m_ref[h:h + 1, :] = m_new
        return carry

    lax.fori_loop(0, nkc, attend_chunk, 0)

    for h in range(N_HEADS):
        hs = slice(h * HEAD_DIM, (h + 1) * HEAD_DIM)
        o = acc_ref[hs, :] / l_ref[h:h + 1, :]
        o_ref[:, hs] = o.T.astype(BF16)


def _dsa(q, qi, wi, k, v, ki, *, q_base, l_valid, topk, causal):
    B, Tq, _ = q.shape
    S = k.shape[1]
    nj, n_kc = Tq // QB, S // KC
    qT = q.transpose(0, 2, 1)
    qiw = qi.reshape(B, nj, QB, IDX_HEADS, IDX_DIM).transpose(0, 1, 4, 3, 2)
    qiw = qiw.reshape(B, nj, IDX_DIM, IDX_HEADS * QB)
    wiw = wi.reshape(B, nj, QB, IDX_HEADS).transpose(0, 1, 3, 2).reshape(B, nj, 1, IDX_HEADS * QB)
    kc = k.reshape(B, n_kc, KC, D_ATTN)
    vT = v.reshape(B, n_kc, KC, D_ATTN).transpose(0, 1, 3, 2)
    kic = ki.reshape(B, n_kc, KC, IDX_DIM)
    kern = functools.partial(_dsa_kernel, q_base=q_base, l_valid=l_valid, topk=topk,
                             causal=causal, n_kc=n_kc)
    return pl.pallas_call(
        kern,
        grid=(B, nj),
        in_specs=[pl.BlockSpec((None, D_ATTN, QB), lambda b, j: (b, 0, j)),
                  pl.BlockSpec((None, None, IDX_DIM, IDX_HEADS * QB), lambda b, j: (b, j, 0, 0)),
                  pl.BlockSpec((None, None, 1, IDX_HEADS * QB), lambda b, j: (b, j, 0, 0)),
                  pl.BlockSpec((None, n_kc, KC, D_ATTN), lambda b, j: (b, 0, 0, 0)),
                  pl.BlockSpec((None, n_kc, D_ATTN, KC), lambda b, j: (b, 0, 0, 0)),
                  pl.BlockSpec((None, n_kc, KC, IDX_DIM), lambda b, j: (b, 0, 0, 0))],
        out_specs=pl.BlockSpec((None, QB, D_ATTN), lambda b, j: (b, j, 0)),
        out_shape=jax.ShapeDtypeStruct((B, Tq, D_ATTN), BF16),
        scratch_shapes=[pltpu.VMEM((n_kc, KC, QB), I32), pltpu.VMEM((n_kc, KC, QB), F32),
                        pltpu.VMEM((D_ATTN, QB), F32), pltpu.VMEM((N_HEADS, QB), F32),
                        pltpu.VMEM((N_HEADS, QB), F32)],
        compiler_params=_cparams(("parallel", "arbitrary")),
        name="dsa",
    )(qT, qiw, wiw, kc, vT, kic)


def _conv_kernel(glu_ref, ctx_ref, w_ref, b_ref, g_ref, be_ref, o_ref, ext_ref, *, T, rt):
    C = glu_ref.shape[-1]
    ext_ref[0:CTX_PAD, :] = ctx_ref[...]
    ext_ref[CTX_PAD:CTX_PAD + T, :] = glu_ref[...]
    ext_ref[CTX_PAD + T:, :] = jnp.zeros((SUBLANES, C), F32)
    first = CTX_PAD - CTX

    def tile(i, carry):
        t0 = pl.multiple_of(i * rt, rt)
        win = ext_ref[pl.ds(t0, rt + CTX_PAD + SUBLANES), :]
        nwin = rt + CTX_PAD + SUBLANES
        shifted = [win if r == 0 else pltpu.roll(win, nwin - r, axis=0) for r in range(SUBLANES)]
        acc = jnp.zeros((rt, C), F32) + b_ref[...]
        for kk in range(CONV_WIDTH):
            a, r = divmod(first + kk, SUBLANES)
            acc = acc + w_ref[kk:kk + 1, :] * shifted[r][SUBLANES * a:SUBLANES * a + rt]
        mu = jnp.mean(acc, axis=-1, keepdims=True)
        xc = acc - mu
        var = jnp.mean(xc * xc, axis=-1, keepdims=True)
        y = xc * lax.rsqrt(var + EPS) * g_ref[...] + be_ref[...]
        o_ref[pl.ds(t0, rt), :] = _silu(y).astype(BF16)
        return carry

    lax.fori_loop(0, T // rt, tile, 0)


def _conv(glu, ctx, dw_w, dw_b, ln_g, ln_b):
    B, T, C = glu.shape
    rt = min(64, T)
    wpad = jnp.pad(dw_w, ((0, CTX_PAD - CONV_WIDTH), (0, 0)))
    kern = functools.partial(_conv_kernel, T=T, rt=rt)
    return pl.pallas_call(
        kern,
        grid=(B,),
        in_specs=[pl.BlockSpec((None, T, C), lambda b: (b, 0, 0)),
                  pl.BlockSpec((None, CTX_PAD, C), lambda b: (b, 0, 0)),
                  _const_spec((CTX_PAD, C)), _const_spec((1, C)), _const_spec((1, C)),
                  _const_spec((1, C))],
        out_specs=pl.BlockSpec((None, T, C), lambda b: (b, 0, 0)),
        out_shape=jax.ShapeDtypeStruct((B, T, C), BF16),
        scratch_shapes=[pltpu.VMEM((T + CTX_PAD + SUBLANES, C), F32)],
        compiler_params=_cparams(("parallel",)),
        name="conv",
    )(glu, ctx, wpad, dw_b.reshape(1, C), ln_g.reshape(1, C), ln_b.reshape(1, C))


def _outproj_kernel(attn_ref, conv_ref, x_ref, g1_ref, sh2_ref, sc2_ref, g2_ref, n2_ref,
                    woa_ref, wob_ref, bo_ref, wr_ref, wsg_ref, wsu_ref, wsd_ref,
                    base_ref, h2_ref, lg_ref):
    mix = (jnp.dot(attn_ref[...], woa_ref[...], preferred_element_type=F32)
           + jnp.dot(conv_ref[...], wob_ref[...], preferred_element_type=F32) + bo_ref[...])
    x1 = x_ref[...] + g1_ref[...] * mix
    y = x1 * lax.rsqrt(jnp.mean(x1 * x1, axis=-1, keepdims=True) + EPS) * n2_ref[...]
    h2 = y * (1.0 + sc2_ref[...]) + sh2_ref[...]
    h2_ref[...] = _pack_pairs(h2)
    hb = h2.astype(BF16)
    lg_ref[...] = jnp.dot(hb, wr_ref[...], preferred_element_type=F32)
    sg = jnp.dot(hb, wsg_ref[...], preferred_element_type=F32)
    su = jnp.dot(hb, wsu_ref[...], preferred_element_type=F32)
    shared = jnp.dot((_silu(sg) * su).astype(BF16), wsd_ref[...], preferred_element_type=F32)
    base_ref[...] = x1 + g2_ref[...] * shared


def _outproj(attn, conv, x, g1, sh2, sc2, g2, n2, woa, wob, bo, wr, wsg, wsu, wsd, *, tm, per_token):
    Bx, T, D = x.shape
    E = wr.shape[1]
    Ds = wsg.shape[1]
    tok = lambda n: pl.BlockSpec((None, tm, n), lambda b, t: (b, t, 0))
    mod = lambda: _mod_spec(per_token, tm, D)
    return pl.pallas_call(
        _outproj_kernel,
        grid=(Bx, T // tm),
        in_specs=[tok(D_ATTN), tok(D - D_ATTN), tok(D), mod(), mod(), mod(), mod(),
                  _const_spec((1, D)), _const_spec((D_ATTN, D)), _const_spec((D - D_ATTN, D)),
                  _const_spec((1, D)), _const_spec((D, E)), _const_spec((D, Ds)),
                  _const_spec((D, Ds)), _const_spec((Ds, D))],
        out_specs=[tok(D), pl.BlockSpec((None, tm, None, D // 2), lambda b, t: (b, t, 0, 0)), tok(E)],
        out_shape=[jax.ShapeDtypeStruct((Bx, T, D), F32), jax.ShapeDtypeStruct((Bx, T, 1, D // 2), U32),
                   jax.ShapeDtypeStruct((Bx, T, E), F32)],
        compiler_params=_cparams(("parallel", "arbitrary")),
        name="outproj",
    )(attn, conv, x, g1, sh2, sc2, g2, n2, woa, wob, bo, wr, wsg, wsu, wsd)


def _route_kernel(lg_ref, bias_ref, idx_ref, w_ref):
    tm, E = lg_ref.shape
    gsz = E // N_GROUPS
    scores = jax.nn.sigmoid(lg_ref[...])
    biased = scores + bias_ref[...]
    lane = lax.broadcasted_iota(I32, (tm, E), 1).astype(F32)
    lane_i = lax.broadcasted_iota(I32, (tm, E), 1)
    in_group = lambda g: (lane_i >= g * gsz) & (lane_i < (g + 1) * gsz)
    ninf = -jnp.inf

    def first_argmax(vals):
        m = jnp.max(vals, axis=-1, keepdims=True)
        i = jnp.min(jnp.where(vals == m, lane, float(E)), axis=-1, keepdims=True)
        return m, i

    gs = []
    for g in range(N_GROUPS):
        mg = jnp.where(in_group(g), biased, ninf)
        m1, i1 = first_argmax(mg)
        m2 = jnp.max(jnp.where(lane == i1, ninf, mg), axis=-1, keepdims=True)
        gs.append(m1 + m2)
    keep = jnp.zeros((tm, E), jnp.bool_)
    for g in range(N_GROUPS):
        rank = jnp.zeros((tm, 1), F32)
        for o in range(N_GROUPS):
            if o == g:
                continue
            beats = (gs[o] > gs[g]) | ((gs[o] == gs[g]) & (o < g))
            rank = rank + jnp.where(beats, 1.0, 0.0)
        keep = keep | (in_group(g) & (rank < float(TOPK_GROUPS)))
    masked = jnp.where(keep, biased, ninf)
    out_lane = lax.broadcasted_iota(I32, (tm, LANES), 1)
    idx_out = jnp.zeros((tm, LANES), F32)
    w_out = jnp.zeros((tm, LANES), F32)
    for kk in range(TOP_K):
        _, i = first_argmax(masked)
        hit = lane == i
        wk = jnp.sum(jnp.where(hit, scores, 0.0), axis=-1, keepdims=True)
        masked = jnp.where(hit, ninf, masked)
        idx_out = jnp.where(out_lane == kk, i, idx_out)
        w_out = jnp.where(out_lane == kk, wk, w_out)
    w_out = w_out / jnp.sum(w_out, axis=-1, keepdims=True) * ROUTED_SCALE
    idx_ref[...] = idx_out.astype(I32)
    w_ref[...] = w_out


def _route(logits, bias):
    M, E = logits.shape
    tm = 256
    return pl.pallas_call(
        _route_kernel,
        grid=(M // tm,),
        in_specs=[pl.BlockSpec((tm, E), lambda i: (i, 0)), _const_spec((1, E))],
        out_specs=[pl.BlockSpec((tm, LANES), lambda i: (i, 0)), pl.BlockSpec((tm, LANES), lambda i: (i, 0))],
        out_shape=[jax.ShapeDtypeStruct((M, LANES), I32), jax.ShapeDtypeStruct((M, LANES), F32)],
        compiler_params=_cparams(("parallel",)),
        name="route",
    )(logits, bias.reshape(1, E))


def _concat_rows_kernel(a_ref, b_ref, o_ref, *, na_blocks):
    i = pl.program_id(0)

    @pl.when(i < na_blocks)
    def _():
        o_ref[...] = a_ref[...]

    @pl.when(i >= na_blocks)
    def _():
        o_ref[...] = b_ref[...]


def _concat_rows(a, b, tm=256):
    na, _, H = a.shape
    nb = b.shape[0]
    nab, nbb = na // tm, nb // tm
    return pl.pallas_call(
        functools.partial(_concat_rows_kernel, na_blocks=nab),
        grid=(nab + nbb,),
        in_specs=[pl.BlockSpec((tm, None, H), lambda i: (jnp.minimum(i, nab - 1), 0, 0)),
                  pl.BlockSpec((tm, None, H), lambda i: (jnp.maximum(i - nab, 0), 0, 0))],
        out_specs=pl.BlockSpec((tm, None, H), lambda i: (i, 0, 0)),
        out_shape=jax.ShapeDtypeStruct((na + nb, 1, H), a.dtype),
        compiler_params=_cparams(("arbitrary",)),
        name="concat_rows",
    )(a, b)


def _moe_kernel(ce_ref, nxt_ref, meta_ref, srcn_ref, srcc_ref, dst_ref, x_hbm, wg_hbm, wu_hbm,
                wd_hbm, y_hbm, xbuf, ybuf, zbuf, xbb, wgs, wus, wds, wgb, wub, wdb, src_s, dst_s,
                gsem, ssem, isem, wsem):
    c = pl.program_id(0)
    n_used = meta_ref[0]
    n_steps = pl.num_programs(0)
    slot = c % 2
    R = MOE_ROWS

    def gather_copy(tok, s, i, u):
        return pltpu.make_async_copy(x_hbm.at[tok], xbuf.at[s, i, pl.ds(u, 1), :], gsem.at[s])

    def scatter_copy(row, s, i, u):
        return pltpu.make_async_copy(ybuf.at[s, i, pl.ds(u, 1), :], y_hbm.at[pl.ds(row, 1), :], ssem.at[s])

    def whole_buffer_wait(buf, sem, s):
        pltpu.make_async_copy(buf.at[1 - s], buf.at[s], sem.at[s]).wait()

    def weight_copies(e):
        return [pltpu.make_async_copy(wg_hbm.at[e], wgs, wsem.at[0]),
                pltpu.make_async_copy(wu_hbm.at[e], wus, wsem.at[1]),
                pltpu.make_async_copy(wd_hbm.at[e], wds, wsem.at[2])]

    def src_copy(vmem_ref):
        return pltpu.make_async_copy(vmem_ref, src_s, isem.at[0])

    def dst_copy():
        return pltpu.make_async_copy(dst_ref, dst_s, isem.at[1])

    def for_rows(fn):
        def body(i, carry):
            for u in range(SUBLANES):
                fn(i, u)
            return carry
        lax.fori_loop(0, R // SUBLANES, body, 0)

    def issue_gather(s):
        for_rows(lambda i, u: gather_copy(src_s[0, 0, i * SUBLANES + u], s, i, u).start())

    @pl.when(c == 0)
    def _():
        for cp in weight_copies(ce_ref[0]):
            cp.start()
        cp = src_copy(srcc_ref)
        cp.start()
        cp.wait()
        issue_gather(0)

    src_copy(srcn_ref).start()
    dst_copy().start()

    @pl.when(c >= n_used)
    def _():
        src_copy(srcn_ref).wait()
        dst_copy().wait()

    @pl.when(c < n_used)
    def _():
        prev = jnp.maximum(c - 1, 0)

        @pl.when((c == 0) | (ce_ref[c] != ce_ref[prev]))
        def _():
            for cp in weight_copies(ce_ref[c]):
                cp.wait()
            for stage, dest in ((wgs, wgb), (wus, wub), (wds, wdb)):
                rows = stage.shape[0] // 8
                for r in range(0, stage.shape[0], rows):
                    dest[r:r + rows, :] = stage[r:r + rows, :].astype(BF16)

            @pl.when(nxt_ref[c] >= 0)
            def _():
                for cp in weight_copies(nxt_ref[c]):
                    cp.start()

        whole_buffer_wait(xbuf, gsem, slot)

        @pl.when(c >= 2)
        def _():
            whole_buffer_wait(ybuf, ssem, slot)

        H = xbuf.shape[-1]
        lo, hi = _unpack_pairs(xbuf[slot].reshape(R, H))
        xbb[:, :H] = lo.astype(BF16)
        xbb[:, H:] = hi.astype(BF16)

        src_copy(srcn_ref).wait()
        dst_copy().wait()

        @pl.when(c + 1 < n_used)
        def _():
            issue_gather(1 - slot)

        xb = xbb[...]
        g = jnp.dot(xb, wgb[...], preferred_element_type=F32)
        u = jnp.dot(xb, wub[...], preferred_element_type=F32)
        y = jnp.dot((_silu(g) * u).astype(BF16), wdb[...], preferred_element_type=F32)
        ybuf[slot] = _pack_pairs(y).reshape(R // SUBLANES, SUBLANES, H)
        for_rows(lambda i, u_: scatter_copy(dst_s[0, 0, i * SUBLANES + u_], slot, i, u_).start())

    @pl.when(c == n_steps - 1)
    def _():
        last = n_used - 1
        whole_buffer_wait(ybuf, ssem, last % 2)

        @pl.when(n_used >= 2)
        def _():
            whole_buffer_wait(ybuf, ssem, (last - 1) % 2)

        zbuf[...] = jnp.zeros(zbuf.shape, U32)
        dump0 = y_hbm.shape[0] - 2 * R
        for s in range(2):
            cp = pltpu.make_async_copy(zbuf, y_hbm.at[pl.ds(dump0 + s * R, R), :], ssem.at[0])
            cp.start()
            cp.wait()


def _moe(h2p, idx, w_gate_e, w_up_e, w_down_e):
    M, _, H = h2p.shape
    E, D, De = w_gate_e.shape
    assert D == 2 * H
    R = MOE_ROWS
    A = M * TOP_K
    nch = A // R + E
    flat_e = idx.reshape(A)
    order = jnp.argsort(flat_e, stable=True).astype(I32)
    experts = jnp.arange(E, dtype=I32)
    counts = jnp.sum((flat_e[None, :] == experts[:, None]).astype(I32), axis=1)
    start = jnp.cumsum(counts).astype(I32) - counts
    nch_e = (counts + R - 1) // R
    ch_end = jnp.cumsum(nch_e).astype(I32)
    ch_start = ch_end - nch_e
    n_used = ch_end[-1]
    cid = jnp.arange(nch, dtype=I32)
    ce = jnp.sum((ch_end[None, :] <= jnp.minimum(cid, n_used - 1)[:, None]).astype(I32), axis=1)
    ce = jnp.minimum(ce, E - 1)
    after = ch_end[ce]
    nxt = jnp.where(after < n_used, ce[jnp.minimum(after, nch - 1)], -1).astype(I32)
    rr = jnp.arange(R, dtype=I32)[None, :]
    local = (cid - ch_start[ce])[:, None] * R + rr
    valid = (cid[:, None] < n_used) & (local < counts[ce][:, None])
    a = order[jnp.clip(start[ce][:, None] + local, 0, A - 1)]
    src = jnp.where(valid, a // TOP_K, 0)
    dst = jnp.where(valid, (a % TOP_K) * M + a // TOP_K, TOP_K * M + (cid[:, None] % 2) * R + rr)
    meta = jnp.stack([n_used, n_used]).astype(I32)

    idx_spec = lambda f: pl.BlockSpec((1, 1, R), f)
    hbm = pl.BlockSpec(memory_space=pl.ANY)
    return pl.pallas_call(
        _moe_kernel,
        grid_spec=pltpu.PrefetchScalarGridSpec(
            num_scalar_prefetch=3,
            grid=(nch,),
            in_specs=[idx_spec(lambda c, *_: (jnp.minimum(c + 1, nch - 1), 0, 0)),
                      idx_spec(lambda c, *_: (c, 0, 0)),
                      idx_spec(lambda c, *_: (c, 0, 0)),
                      hbm, hbm, hbm, hbm],
            out_specs=hbm,
            scratch_shapes=[pltpu.VMEM((2, R // SUBLANES, SUBLANES, H), U32),
                            pltpu.VMEM((2, R // SUBLANES, SUBLANES, H), U32), pltpu.VMEM((R, H), U32),
                            pltpu.VMEM((R, D), BF16), pltpu.VMEM((D, De), F32), pltpu.VMEM((D, De), F32), pltpu.VMEM((De, D), F32),
                            pltpu.VMEM((D, De), BF16), pltpu.VMEM((D, De), BF16),
                            pltpu.VMEM((De, D), BF16),
                            pltpu.SMEM((1, 1, R), I32), pltpu.SMEM((1, 1, R), I32),
                            pltpu.SemaphoreType.DMA((2,)), pltpu.SemaphoreType.DMA((2,)),
                            pltpu.SemaphoreType.DMA((2,)), pltpu.SemaphoreType.DMA((3,))]),
        out_shape=jax.ShapeDtypeStruct((TOP_K * M + 2 * R, H), U32),
        compiler_params=_cparams(("arbitrary",)),
        name="moe",
    )(ce, nxt, meta, src.reshape(nch, 1, R), src.reshape(nch, 1, R), dst.reshape(nch, 1, R),
      h2p, w_gate_e, w_up_e, w_down_e)


def _combine_kernel(base_ref, g2_ref, gw_ref, *refs):
    o_ref = refs[-1]
    H = o_ref.shape[-1] // 2
    gates = [gw_ref[:, k:k + 1] for k in range(TOP_K)]
    for j in range(0, H, LANES):
        halves = [_unpack_pairs(r[:, j:j + LANES]) for r in refs[:-1]]
        for part, sl in ((0, slice(j, j + LANES)), (1, slice(H + j, H + j + LANES))):
            routed = _tree_sum([h[part] * w for h, w in zip(halves, gates)])
            o_ref[:, sl] = base_ref[:, sl] + g2_ref[:, sl] * routed


def _combine(base, g2, y, gw, *, m_total, tok0, tm, per_token):
    Bx, T, D = base.shape
    nt = T // tm

    def plane(kk):
        off = (kk * m_total + tok0) // tm
        return pl.BlockSpec((tm, D // 2), lambda b, t: (off + b * nt + t, 0))

    return pl.pallas_call(
        _combine_kernel,
        grid=(Bx, nt),
        in_specs=[pl.BlockSpec((None, tm, D), lambda b, t: (b, t, 0)), _mod_spec(per_token, tm, D),
                  pl.BlockSpec((tm, LANES), lambda b, t: (tok0 // tm + b * nt + t, 0))]
                 + [plane(kk) for kk in range(TOP_K)],
        out_specs=pl.BlockSpec((None, tm, D), lambda b, t: (b, t, 0)),
        out_shape=jax.ShapeDtypeStruct((Bx, T, D), F32),
        compiler_params=_cparams(("parallel", "arbitrary")),
        name="combine",
    )(base, g2, gw, *([y] * TOP_K))


def _repack_w_in(w_in, b_in):
    c_ki = 3 * D_ATTN + IDX_HEADS * IDX_DIM
    c_wi = c_ki + IDX_DIM
    c_uv = c_wi + IDX_HEADS

    def pack(a):
        z = lambda n: jnp.zeros(a.shape[:-1] + (n,), a.dtype)
        return jnp.concatenate([a[..., :c_wi], z(C_WI - C_KI - IDX_DIM), a[..., c_wi:c_uv],
                                z(C_UV - C_WI - IDX_HEADS), a[..., c_uv:]], axis=-1)

    return pack(w_in).astype(BF16), pack(b_in[None, :])


def _pad_rows(a, n):
    return jnp.pad(a, ((0, 0), (0, n - a.shape[1]), (0, 0)))


def kernel(x_prompt, x_sample, cache_k, cache_v, cache_kidx, state_conv, c_prompt, c_sample, norm1_g, norm2_g, w_ada, b_ada, w_in, b_in, q_norm_g, k_norm_g, idx_k_ln_g, idx_k_ln_b, dw_w, dw_b, conv_ln_g, conv_ln_b, w_out, b_out, w_router, router_bias, w_gate_e, w_up_e, w_down_e, w_sh_gate, w_sh_up, w_sh_down):
    B, T, D = x_prompt.shape
    Bd, Td, _ = x_sample.shape
    depth = w_in.shape[0]
    past = cache_k.shape[2]
    Ms, Mp = Bd * Td, B * T
    M = Mp + Ms
    C = D - D_ATTN
    topk_p = min(IDX_TOPK_MAX, T // 4)
    topk_s = min(IDX_TOPK_MAX, (past + Td) // 4)
    l_s = past + Td
    s_pad = -(-l_s // KC) * KC
    tm_p, tm_s = 256, 128
    xp, xs = x_prompt, x_sample.reshape(1, Ms, D)
    outs = [[] for _ in range(8)]
    for l in range(depth):
        mods = _ada(jnp.concatenate([c_prompt, c_sample], axis=0), w_ada[l], b_ada[l])
        mp = [m[:, None, :] for m in jnp.split(mods[:B], 6, axis=-1)]
        ms = [jnp.repeat(m, Td, axis=0)[None] for m in jnp.split(mods[B:], 6, axis=-1)]
        w2, b2 = _repack_w_in(w_in[l], b_in[l])
        row = lambda a: a.reshape(1, -1)
        proj_w = (row(norm1_g[l]), w2, b2, row(q_norm_g[l]), row(k_norm_g[l]),
                  row(idx_k_ln_g[l]), row(idx_k_ln_b[l]))
        woa, wob = w_out[l, :D_ATTN].astype(BF16), w_out[l, D_ATTN:].astype(BF16)
        out_w = (row(norm2_g[l]), woa, wob, row(b_out[l]), w_router[l].astype(BF16),
                 w_sh_gate[l].astype(BF16), w_sh_up[l].astype(BF16), w_sh_down[l].astype(BF16))

        q, k, kb, v, vb, qi, ki, kib, wi, glu = _inproj(xp, mp[0], mp[1], *proj_w, tm=tm_p, per_token=False)
        attn = _dsa(q, qi, wi, kb, vb, kib, q_base=0, l_valid=T, topk=topk_p, causal=True)
        conv = _conv(glu, jnp.zeros((B, CTX_PAD, C), F32), dw_w[l], dw_b[l], conv_ln_g[l], conv_ln_b[l])
        base_p, h2_p, lg_p = _outproj(attn, conv, xp, mp[2], mp[3], mp[4], mp[5], *out_w,
                                      tm=tm_p, per_token=False)
        outs[0].append(k.reshape(B, T, N_HEADS, HEAD_DIM))
        outs[1].append(v.reshape(B, T, N_HEADS, HEAD_DIM))
        outs[2].append(ki)
        outs[3].append(glu[:, T - CTX:])

        q, k, kb, v, vb, qi, ki, kib, wi, glu = _inproj(xs, ms[0], ms[1], *proj_w, tm=tm_s, per_token=True)
        per_b = lambda a: a.reshape(Bd, Td, a.shape[-1])
        cat = lambda cache, new: _pad_rows(
            jnp.concatenate([cache.reshape(Bd, past, -1).astype(BF16), per_b(new)], axis=1), s_pad)
        attn = _dsa(_pad_rows(per_b(q), QB), _pad_rows(per_b(qi), QB), _pad_rows(per_b(wi), QB),
                    cat(cache_k[l], kb), cat(cache_v[l], vb), cat(cache_kidx[l], kib),
                    q_base=past, l_valid=l_s, topk=topk_s, causal=False)
        attn = attn[:, :Td].reshape(1, Ms, D_ATTN)
        ext = jnp.concatenate([state_conv[l], per_b(glu)], axis=1)
        ctx = jnp.pad(state_conv[l], ((0, 0), (CTX_PAD - CTX, 0), (0, 0)))
        conv = _conv(per_b(glu), ctx, dw_w[l], dw_b[l], conv_ln_g[l], conv_ln_b[l]).reshape(1, Ms, C)
        base_s, h2_s, lg_s = _outproj(attn, conv, xs, ms[2], ms[3], ms[4], ms[5], *out_w,
                                      tm=tm_s, per_token=True)
        outs[4].append(per_b(k).reshape(Bd, Td, N_HEADS, HEAD_DIM))
        outs[5].append(per_b(v).reshape(Bd, Td, N_HEADS, HEAD_DIM))
        outs[6].append(per_b(ki))
        outs[7].append(ext[:, Td:])

        h2 = _concat_rows(h2_p.reshape(Mp, 1, D // 2), h2_s.reshape(Ms, 1, D // 2))
        lg = jnp.concatenate([lg_p.reshape(Mp, -1), lg_s.reshape(Ms, -1)], axis=0)
        idx, gw = _route(lg, router_bias[l])
        y = _moe(h2, idx[:, :TOP_K], w_gate_e[l], w_up_e[l], w_down_e[l])
        xp = _combine(base_p, mp[5], y, gw, m_total=M, tok0=0, tm=tm_p, per_token=False)
        xs = _combine(base_s, ms[5], y, gw, m_total=M, tok0=Mp, tm=256, per_token=True)
    return (xp, xs.reshape(Bd, Td, D), *[jnp.stack(o) for o in outs])
```

```python
import functools

import numpy as np
import jax
import jax.numpy as jnp
from jax import lax
from jax.experimental import pallas as pl
from jax.experimental.pallas import tpu as pltpu

F32, BF16, I32, U32 = jnp.float32, jnp.bfloat16, jnp.int32, jnp.uint32

N_HEADS = 8
HEAD_DIM = 128
D_ATTN = N_HEADS * HEAD_DIM
IDX_HEADS = 16
IDX_DIM = 64
CHUNK = 64
CHUNK_SHIFT = 6
CONV_WIDTH = 31
CTX = CONV_WIDTH - 1
IDX_TOPK_MAX = 256
N_GROUPS = 8
TOPK_GROUPS = 4
TOP_K = 8
ROUTED_SCALE = 2.5
EPS = 1e-6
NEG_INF = -1e30

LANES = 128
SUBLANES = 8
VMEM_LIMIT = 56 * 1024 * 1024

QB = 128
KC = 256
CTX_PAD = 32
MOE_ROWS = 512
ROW_TILES = 16


def _cparams(sem):
    return pltpu.CompilerParams(dimension_semantics=sem, vmem_limit_bytes=VMEM_LIMIT)


def _const_spec(shape):
    nd = len(shape)
    return pl.BlockSpec(shape, lambda *_: (0,) * nd, pipeline_mode=pl.Buffered(1))


def _silu(x):
    return x * jax.nn.sigmoid(x)


def _pack_pairs(x):
    h = x.shape[-1] // 2
    bits = lax.bitcast_convert_type(x.astype(BF16).astype(F32), U32)
    return (bits[:, h:] & jnp.uint32(0xFFFF0000)) | (bits[:, :h] >> 16)


def _unpack_pairs(p):
    lo = lax.bitcast_convert_type(p << 16, F32)
    hi = lax.bitcast_convert_type(p & jnp.uint32(0xFFFF0000), F32)
    return lo, hi


def _tree_sum(parts):
    while len(parts) > 1:
        parts = [a + b for a, b in zip(parts[::2], parts[1::2])] + parts[len(parts) & ~1:]
    return parts[0]


def _ada_kernel(c_ref, w_ref, b_ref, o_ref):
    a = _silu(c_ref[...]).astype(BF16)
    o_ref[...] = jnp.dot(a, w_ref[...].astype(BF16), preferred_element_type=F32) + b_ref[...]


def _ada(c_all, w_ada, b_ada):
    R, D = c_all.shape
    N = w_ada.shape[1]
    TN = 1024
    return pl.pallas_call(
        _ada_kernel,
        grid=(N // TN,),
        in_specs=[pl.BlockSpec((R, D), lambda n: (0, 0)),
                  pl.BlockSpec((D, TN), lambda n: (0, n)),
                  pl.BlockSpec((1, TN), lambda n: (0, n))],
        out_specs=pl.BlockSpec((R, TN), lambda n: (0, n)),
        out_shape=jax.ShapeDtypeStruct((R, N), F32),
        compiler_params=_cparams(("arbitrary",)),
        name="ada",
    )(c_all, w_ada, b_ada.reshape(1, N))


C_Q, C_K, C_V, C_QI = 0, 1024, 2048, 3072
C_KI, C_WI, C_UV, C_UG, C_END = 4096, 4224, 4352, 5376, 6400


def _inproj_kernel(x_ref, sh_ref, sc_ref, g_ref, w_ref, b_ref, qg_ref, kg_ref, lg_ref, lb_ref,
                   q_ref, k_ref, kb_ref, v_ref, vb_ref, qi_ref, ki_ref, kib_ref, wi_ref, glu_ref):
    x = x_ref[...]
    y = x * lax.rsqrt(jnp.mean(x * x, axis=-1, keepdims=True) + EPS) * g_ref[...]
    hb = (y * (1.0 + sc_ref[...]) + sh_ref[...]).astype(BF16)

    def proj(lo, hi):
        return jnp.dot(hb, w_ref[:, lo:hi], preferred_element_type=F32) + b_ref[:, lo:hi]

    def head_norm(z, gain):
        return z * lax.rsqrt(jnp.mean(z * z, axis=-1, keepdims=True) + EPS) * gain

    zq = proj(C_Q, C_K)
    zk = proj(C_K, C_V)
    for h in range(N_HEADS):
        sl = slice(h * HEAD_DIM, (h + 1) * HEAD_DIM)
        q_ref[:, sl] = head_norm(zq[:, sl], qg_ref[...]).astype(BF16)
        kn = head_norm(zk[:, sl], kg_ref[...])
        k_ref[:, sl] = kn
        kb_ref[:, sl] = kn.astype(BF16)
    zv = proj(C_V, C_QI)
    v_ref[...] = zv
    vb_ref[...] = zv.astype(BF16)
    qi_ref[...] = (proj(C_QI, C_KI) * (IDX_DIM ** -0.5)).astype(BF16)
    zki = proj(C_KI, C_WI)[:, :IDX_DIM]
    mu = jnp.mean(zki, axis=-1, keepdims=True)
    xc = zki - mu
    var = jnp.mean(xc * xc, axis=-1, keepdims=True)
    ki = xc * lax.rsqrt(var + EPS) * lg_ref[...] + lb_ref[...]
    ki_ref[...] = ki
    kib_ref[...] = ki.astype(BF16)
    wi_ref[...] = proj(C_WI, C_UV)[:, :IDX_HEADS] * (IDX_HEADS ** -0.5)
    glu_ref[...] = proj(C_UV, C_UG) * jax.nn.sigmoid(proj(C_UG, C_END))


def _mod_spec(per_token, tm, d):
    if per_token:
        return pl.BlockSpec((None, tm, d), lambda b, t: (b, t, 0))
    return pl.BlockSpec((None, 1, d), lambda b, t: (b, 0, 0))


def _inproj(x, sh, sc, g, w2, b2, qg, kg, lg, lb, *, tm, per_token):
    Bx, T, D = x.shape
    tok = lambda n: pl.BlockSpec((None, tm, n), lambda b, t: (b, t, 0))
    shp = lambda n, dt: jax.ShapeDtypeStruct((Bx, T, n), dt)
    return pl.pallas_call(
        _inproj_kernel,
        grid=(Bx, T // tm),
        in_specs=[tok(D), _mod_spec(per_token, tm, D), _mod_spec(per_token, tm, D),
                  _const_spec((1, D)), _const_spec((D, C_END)), _const_spec((1, C_END)),
                  _const_spec((1, HEAD_DIM)), _const_spec((1, HEAD_DIM)),
                  _const_spec((1, IDX_DIM)), _const_spec((1, IDX_DIM))],
        out_specs=[tok(D_ATTN), tok(D_ATTN), tok(D_ATTN), tok(D_ATTN), tok(D_ATTN),
                   tok(IDX_HEADS * IDX_DIM), tok(IDX_DIM), tok(IDX_DIM), tok(IDX_HEADS),
                   tok(D - D_ATTN)],
        out_shape=[shp(D_ATTN, BF16), shp(D_ATTN, F32), shp(D_ATTN, BF16), shp(D_ATTN, F32),
                   shp(D_ATTN, BF16), shp(IDX_HEADS * IDX_DIM, BF16), shp(IDX_DIM, F32),
                   shp(IDX_DIM, BF16), shp(IDX_HEADS, F32), shp(D - D_ATTN, F32)],
        compiler_params=_cparams(("parallel", "arbitrary")),
        name="inproj",
    )(x, sh, sc, g, w2, b2, qg, kg, lg, lb)


def _alibi_slopes():
    return [float(np.float32(2.0) ** np.float32(-8.0 * h / N_HEADS)) for h in range(1, N_HEADS + 1)]


def _dsa_kernel(qT_ref, qiw_ref, wiw_ref, k_ref, vT_ref, ki_ref, o_ref,
                key_ref, sel_ref, acc_ref, m_ref, l_ref, *, q_base, l_valid, topk, causal, n_kc):
    jb = pl.program_id(1)
    nkc = (jb * QB + QB + KC - 1) // KC if causal else n_kc
    lane = lax.broadcasted_iota(I32, (1, QB), 1)
    qpos = q_base + jb * QB + lane
    qchunk = qpos >> CHUNK_SHIFT
    sub = lax.broadcasted_iota(I32, (KC, 1), 0)

    def visible(c):
        kpos = c * KC + sub
        return ((kpos >> CHUNK_SHIFT) <= qchunk) & (kpos < l_valid)

    def score_chunk(c, carry):
        kic = ki_ref[c]
        acc = jnp.zeros((KC, QB), F32)
        for hp in range(IDX_HEADS // 2):
            sl = slice(hp * 2 * QB, (hp + 1) * 2 * QB)
            r = jnp.dot(kic, qiw_ref[:, sl], preferred_element_type=F32)
            t = jnp.maximum(r, 0.0) * wiw_ref[:, sl]
            acc = acc + t[:, :QB] + t[:, QB:]
        bits = pltpu.bitcast(jnp.where(visible(c), acc, -jnp.inf), I32)
        key_ref[c] = bits ^ ((bits >> 31) & 0x7FFFFFFF)
        return carry

    lax.fori_loop(0, nkc, score_chunk, 0)

    def count(pred):
        def body(c, cnt):
            hit = jnp.where(pred(key_ref[c]), 1.0, 0.0)
            return cnt + _tree_sum([hit[i:i + SUBLANES] for i in range(0, KC, SUBLANES)])
        cnt = lax.fori_loop(0, nkc, body, jnp.zeros((SUBLANES, QB), F32))
        return jnp.sum(cnt, axis=0, keepdims=True)

    kf = float(topk)
    int_min = jnp.full((1, QB), -2 ** 31, I32)
    zero = jnp.zeros((1, QB), I32)
    thr0 = jnp.where(count(lambda key: key >= zero) >= kf, zero, int_min)

    def bisect(i, lo):
        cand = lo | jnp.left_shift(jnp.int32(1), 30 - i)
        return jnp.where(count(lambda key: key >= cand) >= kf, cand, lo)

    thr = lax.fori_loop(0, 31, bisect, thr0)
    need = kf - count(lambda key: key > thr)

    tri = (lax.broadcasted_iota(I32, (KC, KC), 0) >= lax.broadcasted_iota(I32, (KC, KC), 1))
    tri = jnp.where(tri, 1.0, 0.0).astype(BF16)

    def select_chunk(c, carry):
        key = key_ref[c]
        tie = key == thr
        tie_f = jnp.where(tie, 1.0, 0.0)
        pref = jnp.dot(tri, tie_f.astype(BF16), preferred_element_type=F32) + carry
        sel = ((key > thr) | (tie & (pref <= need))) & visible(c)
        sel_ref[c] = jnp.where(sel, 1.0, 0.0)
        return carry + jnp.sum(tie_f, axis=0, keepdims=True)

    lax.fori_loop(0, nkc, select_chunk, jnp.zeros((1, QB), F32))

    m_ref[...] = jnp.full((N_HEADS, QB), 0.1 * NEG_INF, F32)
    l_ref[...] = jnp.zeros((N_HEADS, QB), F32)
    acc_ref[...] = jnp.zeros((D_ATTN, QB), F32)
    scale = HEAD_DIM ** -0.5
    slopes = [s / scale for s in _alibi_slopes()]
    c2 = scale * float(np.log2(np.e))

    def attend_chunk(c, carry):
        dist = jnp.abs(qpos - (c * KC + sub)).astype(F32)
        selc = sel_ref[c] > 0.5
        for h in range(N_HEADS):
            hs = slice(h * HEAD_DIM, (h + 1) * HEAD_DIM)
            lt = jnp.dot(k_ref[c, :, hs], qT_ref[hs, :], preferred_element_type=F32)
            lt = jnp.where(selc, lt - slopes[h] * dist, NEG_INF)
            m_old = m_ref[h:h + 1, :]
            m_new = jnp.maximum(m_old, jnp.max(lt, axis=0, keepdims=True))
            alpha = jnp.exp2((m_old - m_new) * c2)
            p = jnp.exp2((lt - m_new) * c2)
            l_ref[h:h + 1, :] = alpha * l_ref[h:h + 1, :] + jnp.sum(p, axis=0, keepdims=True)
            pv = jnp.dot(vT_ref[c, hs, :], p.astype(BF16), preferred_element_type=F32)
            acc_ref[hs, :] = acc_ref[hs, :] * alpha + pv
            m_ref[h:h + 1, :] = m_new
        return carry

    lax.fori_loop(0, nkc, attend_chunk, 0)

    for h in range(N_HEADS):
        hs = slice(h * HEAD_DIM, (h + 1) * HEAD_DIM)
        o = acc_ref[hs, :] / l_ref[h:h + 1, :]
        o_ref[:, hs] = o.T.astype(BF16)


def _dsa(q, qi, wi, k, v, ki, *, q_base, l_valid, topk, causal):
    B, Tq, _ = q.shape
    S = k.shape[1]
    nj, n_kc = Tq // QB, S // KC
    qT = q.transpose(0, 2, 1)
    qiw = qi.reshape(B, nj, QB, IDX_HEADS, IDX_DIM).transpose(0, 1, 4, 3, 2)
    qiw = qiw.reshape(B, nj, IDX_DIM, IDX_HEADS * QB)
    wiw = wi.reshape(B, nj, QB, IDX_HEADS).transpose(0, 1, 3, 2).reshape(B, nj, 1, IDX_HEADS * QB)
    kc = k.reshape(B, n_kc, KC, D_ATTN)
    vT = v.reshape(B, n_kc, KC, D_ATTN).transpose(0, 1, 3, 2)
    kic = ki.reshape(B, n_kc, KC, IDX_DIM)
    kern = functools.partial(_dsa_kernel, q_base=q_base, l_valid=l_valid, topk=topk,
                             causal=causal, n_kc=n_kc)
    return pl.pallas_call(
        kern,
        grid=(B, nj),
        in_specs=[pl.BlockSpec((None, D_ATTN, QB), lambda b, j: (b, 0, j)),
                  pl.BlockSpec((None, None, IDX_DIM, IDX_HEADS * QB), lambda b, j: (b, j, 0, 0)),
                  pl.BlockSpec((None, None, 1, IDX_HEADS * QB), lambda b, j: (b, j, 0, 0)),
                  pl.BlockSpec((None, n_kc, KC, D_ATTN), lambda b, j: (b, 0, 0, 0)),
                  pl.BlockSpec((None, n_kc, D_ATTN, KC), lambda b, j: (b, 0, 0, 0)),
                  pl.BlockSpec((None, n_kc, KC, IDX_DIM), lambda b, j: (b, 0, 0, 0))],
        out_specs=pl.BlockSpec((None, QB, D_ATTN), lambda b, j: (b, j, 0)),
        out_shape=jax.ShapeDtypeStruct((B, Tq, D_ATTN), BF16),
        scratch_shapes=[pltpu.VMEM((n_kc, KC, QB), I32), pltpu.VMEM((n_kc, KC, QB), F32),
                        pltpu.VMEM((D_ATTN, QB), F32), pltpu.VMEM((N_HEADS, QB), F32),
                        pltpu.VMEM((N_HEADS, QB), F32)],
        compiler_params=_cparams(("parallel", "arbitrary")),
        name="dsa",
    )(qT, qiw, wiw, kc, vT, kic)


def _conv_kernel(glu_ref, ctx_ref, w_ref, b_ref, g_ref, be_ref, o_ref, ext_ref, *, T, rt):
    C = glu_ref.shape[-1]
    ext_ref[0:CTX_PAD, :] = ctx_ref[...]
    ext_ref[CTX_PAD:CTX_PAD + T, :] = glu_ref[...]
    ext_ref[CTX_PAD + T:, :] = jnp.zeros((SUBLANES, C), F32)
    first = CTX_PAD - CTX

    def tile(i, carry):
        t0 = pl.multiple_of(i * rt, rt)
        win = ext_ref[pl.ds(t0, rt + CTX_PAD + SUBLANES), :]
        nwin = rt + CTX_PAD + SUBLANES
        shifted = [win if r == 0 else pltpu.roll(win, nwin - r, axis=0) for r in range(SUBLANES)]
        acc = jnp.zeros((rt, C), F32) + b_ref[...]
        for kk in range(CONV_WIDTH):
            a, r = divmod(first + kk, SUBLANES)
            acc = acc + w_ref[kk:kk + 1, :] * shifted[r][SUBLANES * a:SUBLANES * a + rt]
        mu = jnp.mean(acc, axis=-1, keepdims=True)
        xc = acc - mu
        var = jnp.mean(xc * xc, axis=-1, keepdims=True)
        y = xc * lax.rsqrt(var + EPS) * g_ref[...] + be_ref[...]
        o_ref[pl.ds(t0, rt), :] = _silu(y).astype(BF16)
        return carry

    lax.fori_loop(0, T // rt, tile, 0)


def _conv(glu, ctx, dw_w, dw_b, ln_g, ln_b):
    B, T, C = glu.shape
    rt = min(64, T)
    wpad = jnp.pad(dw_w, ((0, CTX_PAD - CONV_WIDTH), (0, 0)))
    kern = functools.partial(_conv_kernel, T=T, rt=rt)
    return pl.pallas_call(
        kern,
        grid=(B,),
        in_specs=[pl.BlockSpec((None, T, C), lambda b: (b, 0, 0)),
                  pl.BlockSpec((None, CTX_PAD, C), lambda b: (b, 0, 0)),
                  _const_spec((CTX_PAD, C)), _const_spec((1, C)), _const_spec((1, C)),
                  _const_spec((1, C))],
        out_specs=pl.BlockSpec((None, T, C), lambda b: (b, 0, 0)),
        out_shape=jax.ShapeDtypeStruct((B, T, C), BF16),
        scratch_shapes=[pltpu.VMEM((T + CTX_PAD + SUBLANES, C), F32)],
        compiler_params=_cparams(("parallel",)),
        name="conv",
    )(glu, ctx, wpad, dw_b.reshape(1, C), ln_g.reshape(1, C), ln_b.reshape(1, C))


def _outproj_kernel(attn_ref, conv_ref, x_ref, g1_ref, sh2_ref, sc2_ref, g2_ref, n2_ref,
                    woa_ref, wob_ref, bo_ref, wr_ref, wsg_ref, wsu_ref, wsd_ref,
                    base_ref, h2_ref, lg_ref):
    mix = (jnp.dot(attn_ref[...], woa_ref[...], preferred_element_type=F32)
           + jnp.dot(conv_ref[...], wob_ref[...], preferred_element_type=F32) + bo_ref[...])
    x1 = x_ref[...] + g1_ref[...] * mix
    y = x1 * lax.rsqrt(jnp.mean(x1 * x1, axis=-1, keepdims=True) + EPS) * n2_ref[...]
    h2 = y * (1.0 + sc2_ref[...]) + sh2_ref[...]
    h2_ref[...] = _pack_pairs(h2)
    hb = h2.astype(BF16)
    lg_ref[...] = jnp.dot(hb, wr_ref[...], preferred_element_type=F32)
    sg = jnp.dot(hb, wsg_ref[...], preferred_element_type=F32)
    su = jnp.dot(hb, wsu_ref[...], preferred_element_type=F32)
    shared = jnp.dot((_silu(sg) * su).astype(BF16), wsd_ref[...], preferred_element_type=F32)
    base_ref[...] = x1 + g2_ref[...] * shared


def _outproj(attn, conv, x, g1, sh2, sc2, g2, n2, woa, wob, bo, wr, wsg, wsu, wsd, *, tm, per_token):
    Bx, T, D = x.shape
    E = wr.shape[1]
    Ds = wsg.shape[1]
    tok = lambda n: pl.BlockSpec((None, tm, n), lambda b, t: (b, t, 0))
    mod = lambda: _mod_spec(per_token, tm, D)
    return pl.pallas_call(
        _outproj_kernel,
        grid=(Bx, T // tm),
        in_specs=[tok(D_ATTN), tok(D - D_ATTN), tok(D), mod(), mod(), mod(), mod(),
                  _const_spec((1, D)), _const_spec((D_ATTN, D)), _const_spec((D - D_ATTN, D)),
                  _const_spec((1, D)), _const_spec((D, E)), _const_spec((D, Ds)),
                  _const_spec((D, Ds)), _const_spec((Ds, D))],
        out_specs=[tok(D), pl.BlockSpec((None, tm, None, D // 2), lambda b, t: (b, t, 0, 0)), tok(E)],
        out_shape=[jax.ShapeDtypeStruct((Bx, T, D), F32), jax.ShapeDtypeStruct((Bx, T, 1, D // 2), U32),
                   jax.ShapeDtypeStruct((Bx, T, E), F32)],
        compiler_params=_cparams(("parallel", "arbitrary")),
        name="outproj",
    )(attn, conv, x, g1, sh2, sc2, g2, n2, woa, wob, bo, wr, wsg, wsu, wsd)


def _route_kernel(lg_ref, bias_ref, idx_ref, w_ref):
    tm, E = lg_ref.shape
    gsz = E // N_GROUPS
    scores = jax.nn.sigmoid(lg_ref[...])
    biased = scores + bias_ref[...]
    lane = lax.broadcasted_iota(I32, (tm, E), 1).astype(F32)
    lane_i = lax.broadcasted_iota(I32, (tm, E), 1)
    in_group = lambda g: (lane_i >= g * gsz) & (lane_i < (g + 1) * gsz)
    ninf = -jnp.inf

    def first_argmax(vals):
        m = jnp.max(vals, axis=-1, keepdims=True)
        i = jnp.min(jnp.where(vals == m, lane, float(E)), axis=-1, keepdims=True)
        return m, i

    gs = []
    for g in range(N_GROUPS):
        mg = jnp.where(in_group(g), biased, ninf)
        m1, i1 = first_argmax(mg)
        m2 = jnp.max(jnp.where(lane == i1, ninf, mg), axis=-1, keepdims=True)
        gs.append(m1 + m2)
    keep = jnp.zeros((tm, E), jnp.bool_)
    for g in range(N_GROUPS):
        rank = jnp.zeros((tm, 1), F32)
        for o in range(N_GROUPS):
            if o == g:
                continue
            beats = (gs[o] > gs[g]) | ((gs[o] == gs[g]) & (o < g))
            rank = rank + jnp.where(beats, 1.0, 0.0)
        keep = keep | (in_group(g) & (rank < float(TOPK_GROUPS)))
    masked = jnp.where(keep, biased, ninf)
    out_lane = lax.broadcasted_iota(I32, (tm, LANES), 1)
    idx_out = jnp.zeros((tm, LANES), F32)
    w_out = jnp.zeros((tm, LANES), F32)
    for kk in range(TOP_K):
        _, i = first_argmax(masked)
        hit = lane == i
        wk = jnp.sum(jnp.where(hit, scores, 0.0), axis=-1, keepdims=True)
        masked = jnp.where(hit, ninf, masked)
        idx_out = jnp.where(out_lane == kk, i, idx_out)
        w_out = jnp.where(out_lane == kk, wk, w_out)
    w_out = w_out / jnp.sum(w_out, axis=-1, keepdims=True) * ROUTED_SCALE
    idx_ref[...] = idx_out.astype(I32)
    w_ref[...] = w_out


def _route(logits, bias):
    M, E = logits.shape
    tm = 256
    return pl.pallas_call(
        _route_kernel,
        grid=(M // tm,),
        in_specs=[pl.BlockSpec((tm, E), lambda i: (i, 0)), _const_spec((1, E))],
        out_specs=[pl.BlockSpec((tm, LANES), lambda i: (i, 0)), pl.BlockSpec((tm, LANES), lambda i: (i, 0))],
        out_shape=[jax.ShapeDtypeStruct((M, LANES), I32), jax.ShapeDtypeStruct((M, LANES), F32)],
        compiler_params=_cparams(("parallel",)),
        name="route",
    )(logits, bias.reshape(1, E))


def _concat_rows_kernel(a_ref, b_ref, o_ref, *, na_blocks):
    i = pl.program_id(0)

    @pl.when(i < na_blocks)
    def _():
        o_ref[...] = a_ref[...]

    @pl.when(i >= na_blocks)
    def _():
        o_ref[...] = b_ref[...]


def _concat_rows(a, b, tm=256):
    na, _, H = a.shape
    nb = b.shape[0]
    nab, nbb = na // tm, nb // tm
    return pl.pallas_call(
        functools.partial(_concat_rows_kernel, na_blocks=nab),
        grid=(nab + nbb,),
        in_specs=[pl.BlockSpec((tm, None, H), lambda i: (jnp.minimum(i, nab - 1), 0, 0)),
                  pl.BlockSpec((tm, None, H), lambda i: (jnp.maximum(i - nab, 0), 0, 0))],
        out_specs=pl.BlockSpec((tm, None, H), lambda i: (i, 0, 0)),
        out_shape=jax.ShapeDtypeStruct((na + nb, 1, H), a.dtype),
        compiler_params=_cparams(("arbitrary",)),
        name="concat_rows",
    )(a, b)


def _moe_kernel(ce_ref, nxt_ref, meta_ref, srcn_ref, srcc_ref, dst_ref, x_hbm, wg_hbm, wu_hbm,
                wd_hbm, y_hbm, xbuf, ybuf, zbuf, xbb, wgs, wus, wds, wgb, wub, wdb, src_s, dst_s,
                gsem, ssem, isem, wsem):
    c = pl.program_id(0)
    n_used = meta_ref[0]
    n_steps = pl.num_programs(0)
    slot = c % 2
    R = MOE_ROWS

    def gather_copy(tok, s, i, u):
        return pltpu.make_async_copy(x_hbm.at[tok], xbuf.at[s, i, pl.ds(u, 1), :], gsem.at[s])

    def scatter_copy(row, s, i, u):
        return pltpu.make_async_copy(ybuf.at[s, i, pl.ds(u, 1), :], y_hbm.at[pl.ds(row, 1), :], ssem.at[s])

    def whole_buffer_wait(buf, sem, s):
        pltpu.make_async_copy(buf.at[1 - s], buf.at[s], sem.at[s]).wait()

    def weight_copies(e):
        return [pltpu.make_async_copy(wg_hbm.at[e], wgs, wsem.at[0]),
                pltpu.make_async_copy(wu_hbm.at[e], wus, wsem.at[1]),
                pltpu.make_async_copy(wd_hbm.at[e], wds, wsem.at[2])]

    def src_copy(vmem_ref):
        return pltpu.make_async_copy(vmem_ref, src_s, isem.at[0])

    def dst_copy():
        return pltpu.make_async_copy(dst_ref, dst_s, isem.at[1])

    def for_rows(fn):
        def body(i, carry):
            for u in range(SUBLANES):
                fn(i, u)
            return carry
        lax.fori_loop(0, R // SUBLANES, body, 0)

    def issue_gather(s):
        for_rows(lambda i, u: gather_copy(src_s[0, 0, i * SUBLANES + u], s, i, u).start())

    @pl.when(c == 0)
    def _():
        for cp in weight_copies(ce_ref[0]):
            cp.start()
        cp = src_copy(srcc_ref)
        cp.start()
        cp.wait()
        issue_gather(0)

    src_copy(srcn_ref).start()
    dst_copy().start()

    @pl.when(c >= n_used)
    def _():
        src_copy(srcn_ref).wait()
        dst_copy().wait()

    @pl.when(c < n_used)
    def _():
        prev = jnp.maximum(c - 1, 0)

        @pl.when((c == 0) | (ce_ref[c] != ce_ref[prev]))
        def _():
            for cp in weight_copies(ce_ref[c]):
                cp.wait()
            for stage, dest in ((wgs, wgb), (wus, wub), (wds, wdb)):
                rows = stage.shape[0] // 8
                for r in range(0, stage.shape[0], rows):
                    dest[r:r + rows, :] = stage[r:r + rows, :].astype(BF16)

            @pl.when(nxt_ref[c] >= 0)
            def _():
                for cp in weight_copies(nxt_ref[c]):
                    cp.start()

        whole_buffer_wait(xbuf, gsem, slot)

        @pl.when(c >= 2)
        def _():
            whole_buffer_wait(ybuf, ssem, slot)

        H = xbuf.shape[-1]
        lo, hi = _unpack_pairs(xbuf[slot].reshape(R, H))
        xbb[:, :H] = lo.astype(BF16)
        xbb[:, H:] = hi.astype(BF16)

        src_copy(srcn_ref).wait()
        dst_copy().wait()

        @pl.when(c + 1 < n_used)
        def _():
            issue_gather(1 - slot)

        xb = xbb[...]
        g = jnp.dot(xb, wgb[...], preferred_element_type=F32)
        u = jnp.dot(xb, wub[...], preferred_element_type=F32)
        y = jnp.dot((_silu(g) * u).astype(BF16), wdb[...], preferred_element_type=F32)
        ybuf[slot] = _pack_pairs(y).reshape(R // SUBLANES, SUBLANES, H)
        for_rows(lambda i, u_: scatter_copy(dst_s[0, 0, i * SUBLANES + u_], slot, i, u_).start())

    @pl.when(c == n_steps - 1)
    def _():
        last = n_used - 1
        whole_buffer_wait(ybuf, ssem, last % 2)

        @pl.when(n_used >= 2)
        def _():
            whole_buffer_wait(ybuf, ssem, (last - 1) % 2)

        zbuf[...] = jnp.zeros(zbuf.shape, U32)
        dump0 = y_hbm.shape[0] - 2 * R
        for s in range(2):
            cp = pltpu.make_async_copy(zbuf, y_hbm.at[pl.ds(dump0 + s * R, R), :], ssem.at[0])
            cp.start()
            cp.wait()


def _moe(h2p, idx, w_gate_e, w_up_e, w_down_e):
    M, _, H = h2p.shape
    E, D, De = w_gate_e.shape
    assert D == 2 * H
    R = MOE_ROWS
    A = M * TOP_K
    nch = A // R + E
    flat_e = idx.reshape(A)
    order = jnp.argsort(flat_e, stable=True).astype(I32)
    experts = jnp.arange(E, dtype=I32)
    counts = jnp.sum((flat_e[None, :] == experts[:, None]).astype(I32), axis=1)
    start = jnp.cumsum(counts).astype(I32) - counts
    nch_e = (counts + R - 1) // R
    ch_end = jnp.cumsum(nch_e).astype(I32)
    ch_start = ch_end - nch_e
    n_used = ch_end[-1]
    cid = jnp.arange(nch, dtype=I32)
    ce = jnp.sum((ch_end[None, :] <= jnp.minimum(cid, n_used - 1)[:, None]).astype(I32), axis=1)
    ce = jnp.minimum(ce, E - 1)
    after = ch_end[ce]
    nxt = jnp.where(after < n_used, ce[jnp.minimum(after, nch - 1)], -1).astype(I32)
    rr = jnp.arange(R, dtype=I32)[None, :]
    local = (cid - ch_start[ce])[:, None] * R + rr
    valid = (cid[:, None] < n_used) & (local < counts[ce][:, None])
    a = order[jnp.clip(start[ce][:, None] + local, 0, A - 1)]
    src = jnp.where(valid, a // TOP_K, 0)
    dst = jnp.where(valid, (a % TOP_K) * M + a // TOP_K, TOP_K * M + (cid[:, None] % 2) * R + rr)
    meta = jnp.stack([n_used, n_used]).astype(I32)

    idx_spec = lambda f: pl.BlockSpec((1, 1, R), f)
    hbm = pl.BlockSpec(memory_space=pl.ANY)
    return pl.pallas_call(
        _moe_kernel,
        grid_spec=pltpu.PrefetchScalarGridSpec(
            num_scalar_prefetch=3,
            grid=(nch,),
            in_specs=[idx_spec(lambda c, *_: (jnp.minimum(c + 1, nch - 1), 0, 0)),
                      idx_spec(lambda c, *_: (c, 0, 0)),
                      idx_spec(lambda c, *_: (c, 0, 0)),
                      hbm, hbm, hbm, hbm],
            out_specs=hbm,
            scratch_shapes=[pltpu.VMEM((2, R // SUBLANES, SUBLANES, H), U32),
                            pltpu.VMEM((2, R // SUBLANES, SUBLANES, H), U32), pltpu.VMEM((R, H), U32),
                            pltpu.VMEM((R, D), BF16), pltpu.VMEM((D, De), F32), pltpu.VMEM((D, De), F32), pltpu.VMEM((De, D), F32),
                            pltpu.VMEM((D, De), BF16), pltpu.VMEM((D, De), BF16),
                            pltpu.VMEM((De, D), BF16),
                            pltpu.SMEM((1, 1, R), I32), pltpu.SMEM((1, 1, R), I32),
                            pltpu.SemaphoreType.DMA((2,)), pltpu.SemaphoreType.DMA((2,)),
                            pltpu.SemaphoreType.DMA((2,)), pltpu.SemaphoreType.DMA((3,))]),
        out_shape=jax.ShapeDtypeStruct((TOP_K * M + 2 * R, H), U32),
        compiler_params=_cparams(("arbitrary",)),
        name="moe",
    )(ce, nxt, meta, src.reshape(nch, 1, R), src.reshape(nch, 1, R), dst.reshape(nch, 1, R),
      h2p, w_gate_e, w_up_e, w_down_e)


def _combine_kernel(base_ref, g2_ref, gw_ref, *refs):
    o_ref = refs[-1]
    H = o_ref.shape[-1] // 2
    gates = [gw_ref[:, k:k + 1] for k in range(TOP_K)]
    for j in range(0, H, LANES):
        halves = [_unpack_pairs(r[:, j:j + LANES]) for r in refs[:-1]]
        for part, sl in ((0, slice(j, j + LANES)), (1, slice(H + j, H + j + LANES))):
            routed = _tree_sum([h[part] * w for h, w in zip(halves, gates)])
            o_ref[:, sl] = base_ref[:, sl] + g2_ref[:, sl] * routed


def _combine(base, g2, y, gw, *, m_total, tok0, tm, per_token):
    Bx, T, D = base.shape
    nt = T // tm

    def plane(kk):
        off = (kk * m_total + tok0) // tm
        return pl.BlockSpec((tm, D // 2), lambda b, t: (off + b * nt + t, 0))

    return pl.pallas_call(
        _combine_kernel,
        grid=(Bx, nt),
        in_specs=[pl.BlockSpec((None, tm, D), lambda b, t: (b, t, 0)), _mod_spec(per_token, tm, D),
                  pl.BlockSpec((tm, LANES), lambda b, t: (tok0 // tm + b * nt + t, 0))]
                 + [plane(kk) for kk in range(TOP_K)],
        out_specs=pl.BlockSpec((None, tm, D), lambda b, t: (b, t, 0)),
        out_shape=jax.ShapeDtypeStruct((Bx, T, D), F32),
        compiler_params=_cparams(("parallel", "arbitrary")),
        name="combine",
    )(base, g2, gw, *([y] * TOP_K))


def _repack_w_in(w_in, b_in):
    c_ki = 3 * D_ATTN + IDX_HEADS * IDX_DIM
    c_wi = c_ki + IDX_DIM
    c_uv = c_wi + IDX_HEADS

    def pack(a):
        z = lambda n: jnp.zeros(a.shape[:-1] + (n,), a.dtype)
        return jnp.concatenate([a[..., :c_wi], z(C_WI - C_KI - IDX_DIM), a[..., c_wi:c_uv],
                                z(C_UV - C_WI - IDX_HEADS), a[..., c_uv:]], axis=-1)

    return pack(w_in).astype(BF16), pack(b_in[None, :])


def _pad_rows(a, n):
    return jnp.pad(a, ((0, 0), (0, n - a.shape[1]), (0, 0)))


def kernel(x_prompt, x_sample, cache_k, cache_v, cache_kidx, state_conv, c_prompt, c_sample, norm1_g, norm2_g, w_ada, b_ada, w_in, b_in, q_norm_g, k_norm_g, idx_k_ln_g, idx_k_ln_b, dw_w, dw_b, conv_ln_g, conv_ln_b, w_out, b_out, w_router, router_bias, w_gate_e, w_up_e, w_down_e, w_sh_gate, w_sh_up, w_sh_down):
    B, T, D = x_prompt.shape
    Bd, Td, _ = x_sample.shape
    depth = w_in.shape[0]
    past = cache_k.shape[2]
    Ms, Mp = Bd * Td, B * T
    M = Mp + Ms
    C = D - D_ATTN
    topk_p = min(IDX_TOPK_MAX, T // 4)
    topk_s = min(IDX_TOPK_MAX, (past + Td) // 4)
    l_s = past + Td
    s_pad = -(-l_s // KC) * KC
    tm_p, tm_s = 256, 128
    xp, xs = x_prompt, x_sample.reshape(1, Ms, D)
    outs = [[] for _ in range(8)]
    for l in range(depth):
        mods = _ada(jnp.concatenate([c_prompt, c_sample], axis=0), w_ada[l], b_ada[l])
        mp = [m[:, None, :] for m in jnp.split(mods[:B], 6, axis=-1)]
        ms = [jnp.repeat(m, Td, axis=0)[None] for m in jnp.split(mods[B:], 6, axis=-1)]
        w2, b2 = _repack_w_in(w_in[l], b_in[l])
        row = lambda a: a.reshape(1, -1)
        proj_w = (row(norm1_g[l]), w2, b2, row(q_norm_g[l]), row(k_norm_g[l]),
                  row(idx_k_ln_g[l]), row(idx_k_ln_b[l]))
        woa, wob = w_out[l, :D_ATTN].astype(BF16), w_out[l, D_ATTN:].astype(BF16)
        out_w = (row(norm2_g[l]), woa, wob, row(b_out[l]), w_router[l].astype(BF16),
                 w_sh_gate[l].astype(BF16), w_sh_up[l].astype(BF16), w_sh_down[l].astype(BF16))

        q, k, kb, v, vb, qi, ki, kib, wi, glu = _inproj(xp, mp[0], mp[1], *proj_w, tm=tm_p, per_token=False)
        attn = _dsa(q, qi, wi, kb, vb, kib, q_base=0, l_valid=T, topk=topk_p, causal=True)
        conv = _conv(glu, jnp.zeros((B, CTX_PAD, C), F32), dw_w[l], dw_b[l], conv_ln_g[l], conv_ln_b[l])
        base_p, h2_p, lg_p = _outproj(attn, conv, xp, mp[2], mp[3], mp[4], mp[5], *out_w,
                                      tm=tm_p, per_token=False)
        outs[0].append(k.reshape(B, T, N_HEADS, HEAD_DIM))
        outs[1].append(v.reshape(B, T, N_HEADS, HEAD_DIM))
        outs[2].append(ki)
        outs[3].append(glu[:, T - CTX:])

        q, k, kb, v, vb, qi, ki, kib, wi, glu = _inproj(xs, ms[0], ms[1], *proj_w, tm=tm_s, per_token=True)
        per_b = lambda a: a.reshape(Bd, Td, a.shape[-1])
        cat = lambda cache, new: _pad_rows(
            jnp.concatenate([cache.reshape(Bd, past, -1).astype(BF16), per_b(new)], axis=1), s_pad)
        attn = _dsa(_pad_rows(per_b(q), QB), _pad_rows(per_b(qi), QB), _pad_rows(per_b(wi), QB),
                    cat(cache_k[l], kb), cat(cache_v[l], vb), cat(cache_kidx[l], kib),
                    q_base=past, l_valid=l_s, topk=topk_s, causal=False)
        attn = attn[:, :Td].reshape(1, Ms, D_ATTN)
        ext = jnp.concatenate([state_conv[l], per_b(glu)], axis=1)
        ctx = jnp.pad(state_conv[l], ((0, 0), (CTX_PAD - CTX, 0), (0, 0)))
        conv = _conv(per_b(glu), ctx, dw_w[l], dw_b[l], conv_ln_g[l], conv_ln_b[l]).reshape(1, Ms, C)
        base_s, h2_s, lg_s = _outproj(attn, conv, xs, ms[2], ms[3], ms[4], ms[5], *out_w,
                                      tm=tm_s, per_token=True)
        outs[4].append(per_b(k).reshape(Bd, Td, N_HEADS, HEAD_DIM))
        outs[5].append(per_b(v).reshape(Bd, Td, N_HEADS, HEAD_DIM))
        outs[6].append(per_b(ki))
        outs[7].append(ext[:, Td:])

        h2 = _concat_rows(h2_p.reshape(Mp, 1, D // 2), h2_s.reshape(Ms, 1, D // 2))
        lg = jnp.concatenate([lg_p.reshape(Mp, -1), lg_s.reshape(Ms, -1)], axis=0)
        idx, gw = _route(lg, router_bias[l])
        y = _moe(h2, idx[:, :TOP_K], w_gate_e[l], w_up_e[l], w_down_e[l])
        xp = _combine(base_p, mp[5], y, gw, m_total=M, tok0=0, tm=tm_p, per_token=False)
        xs = _combine(base_s, ms[5], y, gw, m_total=M, tok0=Mp, tm=256, per_token=True)
    return (xp, xs.reshape(Bd, Td, D), *[jnp.stack(o) for o in outs])
```

```python
import functools

import numpy as np
import jax
import jax.numpy as jnp
from jax import lax
from jax.experimental import pallas as pl
from jax.experimental.pallas import tpu as pltpu

F32, BF16, I32, U32 = jnp.float32, jnp.bfloat16, jnp.int32, jnp.uint32

N_HEADS = 8
HEAD_DIM = 128
D_ATTN = N_HEADS * HEAD_DIM
IDX_HEADS = 16
IDX_DIM = 64
CHUNK = 64
CHUNK_SHIFT = 6
CONV_WIDTH = 31
CTX = CONV_WIDTH - 1
IDX_TOPK_MAX = 256
N_GROUPS = 8
TOPK_GROUPS = 4
TOP_K = 8
ROUTED_SCALE = 2.5
EPS = 1e-6
NEG_INF = -1e30

LANES = 128
SUBLANES = 8
VMEM_LIMIT = 56 * 1024 * 1024

QB = 128
KC = 256
CTX_PAD = 32
MOE_ROWS = 256
ROW_TILES = 16


def _cparams(sem):
    return pltpu.CompilerParams(dimension_semantics=sem, vmem_limit_bytes=VMEM_LIMIT)


def _const_spec(shape):
    nd = len(shape)
    return pl.BlockSpec(shape, lambda *_: (0,) * nd, pipeline_mode=pl.Buffered(1))


def _silu(x):
    return x * jax.nn.sigmoid(x)


def _pack_pairs(x):
    h = x.shape[-1] // 2
    bits = lax.bitcast_convert_type(x.astype(BF16).astype(F32), U32)
    return (bits[:, h:] & jnp.uint32(0xFFFF0000)) | (bits[:, :h] >> 16)


def _unpack_pairs(p):
    lo = lax.bitcast_convert_type(p << 16, F32)
    hi = lax.bitcast_convert_type(p & jnp.uint32(0xFFFF0000), F32)
    return lo, hi


def _tree_sum(parts):
    while len(parts) > 1:
        parts = [a + b for a, b in zip(parts[::2], parts[1::2])] + parts[len(parts) & ~1:]
    return parts[0]


def _ada_kernel(c_ref, w_ref, b_ref, o_ref):
    a = _silu(c_ref[...]).astype(BF16)
    o_ref[...] = jnp.dot(a, w_ref[...].astype(BF16), preferred_element_type=F32) + b_ref[...]


def _ada(c_all, w_ada, b_ada):
    R, D = c_all.shape
    N = w_ada.shape[1]
    TN = 1024
    return pl.pallas_call(
        _ada_kernel,
        grid=(N // TN,),
        in_specs=[pl.BlockSpec((R, D), lambda n: (0, 0)),
                  pl.BlockSpec((D, TN), lambda n: (0, n)),
                  pl.BlockSpec((1, TN), lambda n: (0, n))],
        out_specs=pl.BlockSpec((R, TN), lambda n: (0, n)),
        out_shape=jax.ShapeDtypeStruct((R, N), F32),
        compiler_params=_cparams(("arbitrary",)),
        name="ada",
    )(c_all, w_ada, b_ada.reshape(1, N))


C_Q, C_K, C_V, C_QI = 0, 1024, 2048, 3072
C_KI, C_WI, C_UV, C_UG, C_END = 4096, 4224, 4352, 5376, 6400


def _inproj_kernel(x_ref, sh_ref, sc_ref, g_ref, w_ref, b_ref, qg_ref, kg_ref, lg_ref, lb_ref,
                   q_ref, k_ref, kb_ref, v_ref, vb_ref, qi_ref, ki_ref, kib_ref, wi_ref, glu_ref):
    x = x_ref[...]
    y = x * lax.rsqrt(jnp.mean(x * x, axis=-1, keepdims=True) + EPS) * g_ref[...]
    hb = (y * (1.0 + sc_ref[...]) + sh_ref[...]).astype(BF16)

    def proj(lo, hi):
        return jnp.dot(hb, w_ref[:, lo:hi], preferred_element_type=F32) + b_ref[:, lo:hi]

    def head_norm(z, gain):
        return z * lax.rsqrt(jnp.mean(z * z, axis=-1, keepdims=True) + EPS) * gain

    zq = proj(C_Q, C_K)
    zk = proj(C_K, C_V)
    for h in range(N_HEADS):
        sl = slice(h * HEAD_DIM, (h + 1) * HEAD_DIM)
        q_ref[:, sl] = head_norm(zq[:, sl], qg_ref[...]).astype(BF16)
        kn = head_norm(zk[:, sl], kg_ref[...])
        k_ref[:, sl] = kn
        kb_ref[:, sl] = kn.astype(BF16)
    zv = proj(C_V, C_QI)
    v_ref[...] = zv
    vb_ref[...] = zv.astype(BF16)
    qi_ref[...] = (proj(C_QI, C_KI) * (IDX_DIM ** -0.5)).astype(BF16)
    zki = proj(C_KI, C_WI)[:, :IDX_DIM]
    mu = jnp.mean(zki, axis=-1, keepdims=True)
    xc = zki - mu
    var = jnp.mean(xc * xc, axis=-1, keepdims=True)
    ki = xc * lax.rsqrt(var + EPS) * lg_ref[...] + lb_ref[...]
    ki_ref[...] = ki
    kib_ref[...] = ki.astype(BF16)
    wi_ref[...] = proj(C_WI, C_UV)[:, :IDX_HEADS] * (IDX_HEADS ** -0.5)
    glu_ref[...] = proj(C_UV, C_UG) * jax.nn.sigmoid(proj(C_UG, C_END))


def _mod_spec(per_token, tm, d):
    if per_token:
        return pl.BlockSpec((None, tm, d), lambda b, t: (b, t, 0))
    return pl.BlockSpec((None, 1, d), lambda b, t: (b, 0, 0))


def _inproj(x, sh, sc, g, w2, b2, qg, kg, lg, lb, *, tm, per_token):
    Bx, T, D = x.shape
    tok = lambda n: pl.BlockSpec((None, tm, n), lambda b, t: (b, t, 0))
    shp = lambda n, dt: jax.ShapeDtypeStruct((Bx, T, n), dt)
    return pl.pallas_call(
        _inproj_kernel,
        grid=(Bx, T // tm),
        in_specs=[tok(D), _mod_spec(per_token, tm, D), _mod_spec(per_token, tm, D),
                  _const_spec((1, D)), _const_spec((D, C_END)), _const_spec((1, C_END)),
                  _const_spec((1, HEAD_DIM)), _const_spec((1, HEAD_DIM)),
                  _const_spec((1, IDX_DIM)), _const_spec((1, IDX_DIM))],
        out_specs=[tok(D_ATTN), tok(D_ATTN), tok(D_ATTN), tok(D_ATTN), tok(D_ATTN),
                   tok(IDX_HEADS * IDX_DIM), tok(IDX_DIM), tok(IDX_DIM), tok(IDX_HEADS),
                   tok(D - D_ATTN)],
        out_shape=[shp(D_ATTN, BF16), shp(D_ATTN, F32), shp(D_ATTN, BF16), shp(D_ATTN, F32),
                   shp(D_ATTN, BF16), shp(IDX_HEADS * IDX_DIM, BF16), shp(IDX_DIM, F32),
                   shp(IDX_DIM, BF16), shp(IDX_HEADS, F32), shp(D - D_ATTN, F32)],
        compiler_params=_cparams(("parallel", "arbitrary")),
        name="inproj",
    )(x, sh, sc, g, w2, b2, qg, kg, lg, lb)


def _alibi_slopes():
    return [float(np.float32(2.0) ** np.float32(-8.0 * h / N_HEADS)) for h in range(1, N_HEADS + 1)]


def _dsa_kernel(qT_ref, qiw_ref, wiw_ref, k_ref, vT_ref, ki_ref, o_ref,
                key_ref, sel_ref, acc_ref, m_ref, l_ref, *, q_base, l_valid, topk, causal, n_kc):
    jb = pl.program_id(1)
    nkc = (jb * QB + QB + KC - 1) // KC if causal else n_kc
    lane = lax.broadcasted_iota(I32, (1, QB), 1)
    qpos = q_base + jb * QB + lane
    qchunk = qpos >> CHUNK_SHIFT
    sub = lax.broadcasted_iota(I32, (KC, 1), 0)

    def visible(c):
        kpos = c * KC + sub
        return ((kpos >> CHUNK_SHIFT) <= qchunk) & (kpos < l_valid)

    def score_chunk(c, carry):
        kic = ki_ref[c]
        acc = jnp.zeros((KC, QB), F32)
        for hp in range(IDX_HEADS // 2):
            sl = slice(hp * 2 * QB, (hp + 1) * 2 * QB)
            r = jnp.dot(kic, qiw_ref[:, sl], preferred_element_type=F32)
            t = jnp.maximum(r, 0.0) * wiw_ref[:, sl]
            acc = acc + t[:, :QB] + t[:, QB:]
        bits = pltpu.bitcast(jnp.where(visible(c), acc, -jnp.inf), I32)
        key_ref[c] = bits ^ ((bits >> 31) & 0x7FFFFFFF)
        return carry

    lax.fori_loop(0, nkc, score_chunk, 0)

    def count(pred):
        def body(c, cnt):
            hit = jnp.where(pred(key_ref[c]), 1.0, 0.0)
            return cnt + _tree_sum([hit[i:i + SUBLANES] for i in range(0, KC, SUBLANES)])
        cnt = lax.fori_loop(0, nkc, body, jnp.zeros((SUBLANES, QB), F32))
        return jnp.sum(cnt, axis=0, keepdims=True)

    kf = float(topk)
    int_min = jnp.full((1, QB), -2 ** 31, I32)
    zero = jnp.zeros((1, QB), I32)
    thr0 = jnp.where(count(lambda key: key >= zero) >= kf, zero, int_min)

    def bisect(i, lo):
        cand = lo | jnp.left_shift(jnp.int32(1), 30 - i)
        return jnp.where(count(lambda key: key >= cand) >= kf, cand, lo)

    thr = lax.fori_loop(0, 31, bisect, thr0)
    need = kf - count(lambda key: key > thr)

    tri = (lax.broadcasted_iota(I32, (KC, KC), 0) >= lax.broadcasted_iota(I32, (KC, KC), 1))
    tri = jnp.where(tri, 1.0, 0.0).astype(BF16)

    def select_chunk(c, carry):
        key = key_ref[c]
        tie = key == thr
        tie_f = jnp.where(tie, 1.0, 0.0)
        pref = jnp.dot(tri, tie_f.astype(BF16), preferred_element_type=F32) + carry
        sel = ((key > thr) | (tie & (pref <= need))) & visible(c)
        sel_ref[c] = jnp.where(sel, 1.0, 0.0)
        return carry + jnp.sum(tie_f, axis=0, keepdims=True)

    lax.fori_loop(0, nkc, select_chunk, jnp.zeros((1, QB), F32))

    m_ref[...] = jnp.full((N_HEADS, QB), 0.1 * NEG_INF, F32)
    l_ref[...] = jnp.zeros((N_HEADS, QB), F32)
    acc_ref[...] = jnp.zeros((D_ATTN, QB), F32)
    scale = HEAD_DIM ** -0.5
    slopes = [s / scale for s in _alibi_slopes()]
    c2 = scale * float(np.log2(np.e))

    def attend_chunk(c, carry):
        dist = jnp.abs(qpos - (c * KC + sub)).astype(F32)
        selc = sel_ref[c] > 0.5
        for h in range(N_HEADS):
            hs = slice(h * HEAD_DIM, (h + 1) * HEAD_DIM)
            lt = jnp.dot(k_ref[c, :, hs], qT_ref[hs, :], preferred_element_type=F32)
            lt = jnp.where(selc, lt - slopes[h] * dist, NEG_INF)
            m_old = m_ref[h:h + 1, :]
            m_new = jnp.maximum(m_old, jnp.max(lt, axis=0, keepdims=True))
            alpha = jnp.exp2((m_old - m_new) * c2)
            p = jnp.exp2((lt - m_new) * c2)
            l_ref[h:h + 1, :] = alpha * l_ref[h:h + 1, :] + jnp.sum(p, axis=0, keepdims=True)
            pv = jnp.dot(vT_ref[c, hs, :], p.astype(BF16), preferred_element_type=F32)
            acc_ref[hs, :] = acc_ref[hs, :] * alpha + pv
            m_ref[h:h + 1, :] = m_new
        return carry

    lax.fori_loop(0, nkc, attend_chunk, 0)

    for h in range(N_HEADS):
        hs = slice(h * HEAD_DIM, (h + 1) * HEAD_DIM)
        o = acc_ref[hs, :] / l_ref[h:h + 1, :]
        o_ref[:, hs] = o.T.astype(BF16)


def _dsa(q, qi, wi, k, v, ki, *, q_base, l_valid, topk, causal):
    B, Tq, _ = q.shape
    S = k.shape[1]
    nj, n_kc = Tq // QB, S // KC
    qT = q.transpose(0, 2, 1)
    qiw = qi.reshape(B, nj, QB, IDX_HEADS, IDX_DIM).transpose(0, 1, 4, 3, 2)
    qiw = qiw.reshape(B, nj, IDX_DIM, IDX_HEADS * QB)
    wiw = wi.reshape(B, nj, QB, IDX_HEADS).transpose(0, 1, 3, 2).reshape(B, nj, 1, IDX_HEADS * QB)
    kc = k.reshape(B, n_kc, KC, D_ATTN)
    vT = v.reshape(B, n_kc, KC, D_ATTN).transpose(0, 1, 3, 2)
    kic = ki.reshape(B, n_kc, KC, IDX_DIM)
    kern = functools.partial(_dsa_kernel, q_base=q_base, l_valid=l_valid, topk=topk,
                             causal=causal, n_kc=n_kc)
    return pl.pallas_call(
        kern,
        grid=(B, nj),
        in_specs=[pl.BlockSpec((None, D_ATTN, QB), lambda b, j: (b, 0, j)),
                  pl.BlockSpec((None, None, IDX_DIM, IDX_HEADS * QB), lambda b, j: (b, j, 0, 0)),
                  pl.BlockSpec((None, None, 1, IDX_HEADS * QB), lambda b, j: (b, j, 0, 0)),
                  pl.BlockSpec((None, n_kc, KC, D_ATTN), lambda b, j: (b, 0, 0, 0)),
                  pl.BlockSpec((None, n_kc, D_ATTN, KC), lambda b, j: (b, 0, 0, 0)),
                  pl.BlockSpec((None, n_kc, KC, IDX_DIM), lambda b, j: (b, 0, 0, 0))],
        out_specs=pl.BlockSpec((None, QB, D_ATTN), lambda b, j: (b, j, 0)),
        out_shape=jax.ShapeDtypeStruct((B, Tq, D_ATTN), BF16),
        scratch_shapes=[pltpu.VMEM((n_kc, KC, QB), I32), pltpu.VMEM((n_kc, KC, QB), F32),
                        pltpu.VMEM((D_ATTN, QB), F32), pltpu.VMEM((N_HEADS, QB), F32),
                        pltpu.VMEM((N_HEADS, QB), F32)],
        compiler_params=_cparams(("parallel", "arbitrary")),
        name="dsa",
    )(qT, qiw, wiw, kc, vT, kic)


def _conv_kernel(glu_ref, ctx_ref, w_ref, b_ref, g_ref, be_ref, o_ref, ext_ref, *, T, rt):
    C = glu_ref.shape[-1]
    ext_ref[0:CTX_PAD, :] = ctx_ref[...]
    ext_ref[CTX_PAD:CTX_PAD + T, :] = glu_ref[...]
    ext_ref[CTX_PAD + T:, :] = jnp.zeros((SUBLANES, C), F32)
    first = CTX_PAD - CTX

    def tile(i, carry):
        t0 = pl.multiple_of(i * rt, rt)
        win = ext_ref[pl.ds(t0, rt + CTX_PAD + SUBLANES), :]
        nwin = rt + CTX_PAD + SUBLANES
        shifted = [win if r == 0 else pltpu.roll(win, nwin - r, axis=0) for r in range(SUBLANES)]
        acc = jnp.zeros((rt, C), F32) + b_ref[...]
        for kk in range(CONV_WIDTH):
            a, r = divmod(first + kk, SUBLANES)
            acc = acc + w_ref[kk:kk + 1, :] * shifted[r][SUBLANES * a:SUBLANES * a + rt]
        mu = jnp.mean(acc, axis=-1, keepdims=True)
        xc = acc - mu
        var = jnp.mean(xc * xc, axis=-1, keepdims=True)
        y = xc * lax.rsqrt(var + EPS) * g_ref[...] + be_ref[...]
        o_ref[pl.ds(t0, rt), :] = _silu(y).astype(BF16)
        return carry

    lax.fori_loop(0, T // rt, tile, 0)


def _conv(glu, ctx, dw_w, dw_b, ln_g, ln_b):
    B, T, C = glu.shape
    rt = min(64, T)
    wpad = jnp.pad(dw_w, ((0, CTX_PAD - CONV_WIDTH), (0, 0)))
    kern = functools.partial(_conv_kernel, T=T, rt=rt)
    return pl.pallas_call(
        kern,
        grid=(B,),
        in_specs=[pl.BlockSpec((None, T, C), lambda b: (b, 0, 0)),
                  pl.BlockSpec((None, CTX_PAD, C), lambda b: (b, 0, 0)),
                  _const_spec((CTX_PAD, C)), _const_spec((1, C)), _const_spec((1, C)),
                  _const_spec((1, C))],
        out_specs=pl.BlockSpec((None, T, C), lambda b: (b, 0, 0)),
        out_shape=jax.ShapeDtypeStruct((B, T, C), BF16),
        scratch_shapes=[pltpu.VMEM((T + CTX_PAD + SUBLANES, C), F32)],
        compiler_params=_cparams(("parallel",)),
        name="conv",
    )(glu, ctx, wpad, dw_b.reshape(1, C), ln_g.reshape(1, C), ln_b.reshape(1, C))


def _outproj_kernel(attn_ref, conv_ref, x_ref, g1_ref, sh2_ref, sc2_ref, g2_ref, n2_ref,
                    woa_ref, wob_ref, bo_ref, wr_ref, wsg_ref, wsu_ref, wsd_ref,
                    base_ref, h2_ref, lg_ref):
    mix = (jnp.dot(attn_ref[...], woa_ref[...], preferred_element_type=F32)
           + jnp.dot(conv_ref[...], wob_ref[...], preferred_element_type=F32) + bo_ref[...])
    x1 = x_ref[...] + g1_ref[...] * mix
    y = x1 * lax.rsqrt(jnp.mean(x1 * x1, axis=-1, keepdims=True) + EPS) * n2_ref[...]
    h2 = y * (1.0 + sc2_ref[...]) + sh2_ref[...]
    h2_ref[...] = _pack_pairs(h2)
    hb = h2.astype(BF16)
    lg_ref[...] = jnp.dot(hb, wr_ref[...], preferred_element_type=F32)
    sg = jnp.dot(hb, wsg_ref[...], preferred_element_type=F32)
    su = jnp.dot(hb, wsu_ref[...], preferred_element_type=F32)
    shared = jnp.dot((_silu(sg) * su).astype(BF16), wsd_ref[...], preferred_element_type=F32)
    base_ref[...] = x1 + g2_ref[...] * shared


def _outproj(attn, conv, x, g1, sh2, sc2, g2, n2, woa, wob, bo, wr, wsg, wsu, wsd, *, tm, per_token):
    Bx, T, D = x.shape
    E = wr.shape[1]
    Ds = wsg.shape[1]
    tok = lambda n: pl.BlockSpec((None, tm, n), lambda b, t: (b, t, 0))
    mod = lambda: _mod_spec(per_token, tm, D)
    return pl.pallas_call(
        _outproj_kernel,
        grid=(Bx, T // tm),
        in_specs=[tok(D_ATTN), tok(D - D_ATTN), tok(D), mod(), mod(), mod(), mod(),
                  _const_spec((1, D)), _const_spec((D_ATTN, D)), _const_spec((D - D_ATTN, D)),
                  _const_spec((1, D)), _const_spec((D, E)), _const_spec((D, Ds)),
                  _const_spec((D, Ds)), _const_spec((Ds, D))],
        out_specs=[tok(D), pl.BlockSpec((None, tm, None, D // 2), lambda b, t: (b, t, 0, 0)), tok(E)],
        out_shape=[jax.ShapeDtypeStruct((Bx, T, D), F32), jax.ShapeDtypeStruct((Bx, T, 1, D // 2), U32),
                   jax.ShapeDtypeStruct((Bx, T, E), F32)],
        compiler_params=_cparams(("parallel", "arbitrary")),
        name="outproj",
    )(attn, conv, x, g1, sh2, sc2, g2, n2, woa, wob, bo, wr, wsg, wsu, wsd)


def _route_kernel(lg_ref, bias_ref, idx_ref, w_ref):
    tm, E = lg_ref.shape
    gsz = E // N_GROUPS
    scores = jax.nn.sigmoid(lg_ref[...])
    biased = scores + bias_ref[...]
    lane = lax.broadcasted_iota(I32, (tm, E), 1).astype(F32)
    lane_i = lax.broadcasted_iota(I32, (tm, E), 1)
    in_group = lambda g: (lane_i >= g * gsz) & (lane_i < (g + 1) * gsz)
    ninf = -jnp.inf

    def first_argmax(vals):
        m = jnp.max(vals, axis=-1, keepdims=True)
        i = jnp.min(jnp.where(vals == m, lane, float(E)), axis=-1, keepdims=True)
        return m, i

    gs = []
    for g in range(N_GROUPS):
        mg = jnp.where(in_group(g), biased, ninf)
        m1, i1 = first_argmax(mg)
        m2 = jnp.max(jnp.where(lane == i1, ninf, mg), axis=-1, keepdims=True)
        gs.append(m1 + m2)
    keep = jnp.zeros((tm, E), jnp.bool_)
    for g in range(N_GROUPS):
        rank = jnp.zeros((tm, 1), F32)
        for o in range(N_GROUPS):
            if o == g:
                continue
            beats = (gs[o] > gs[g]) | ((gs[o] == gs[g]) & (o < g))
            rank = rank + jnp.where(beats, 1.0, 0.0)
        keep = keep | (in_group(g) & (rank < float(TOPK_GROUPS)))
    masked = jnp.where(keep, biased, ninf)
    out_lane = lax.broadcasted_iota(I32, (tm, LANES), 1)
    idx_out = jnp.zeros((tm, LANES), F32)
    w_out = jnp.zeros((tm, LANES), F32)
    for kk in range(TOP_K):
        _, i = first_argmax(masked)
        hit = lane == i
        wk = jnp.sum(jnp.where(hit, scores, 0.0), axis=-1, keepdims=True)
        masked = jnp.where(hit, ninf, masked)
        idx_out = jnp.where(out_lane == kk, i, idx_out)
        w_out = jnp.where(out_lane == kk, wk, w_out)
    w_out = w_out / jnp.sum(w_out, axis=-1, keepdims=True) * ROUTED_SCALE
    idx_ref[...] = idx_out.astype(I32)
    w_ref[...] = w_out


def _route(logits, bias):
    M, E = logits.shape
    tm = 256
    return pl.pallas_call(
        _route_kernel,
        grid=(M // tm,),
        in_specs=[pl.BlockSpec((tm, E), lambda i: (i, 0)), _const_spec((1, E))],
        out_specs=[pl.BlockSpec((tm, LANES), lambda i: (i, 0)), pl.BlockSpec((tm, LANES), lambda i: (i, 0))],
        out_shape=[jax.ShapeDtypeStruct((M, LANES), I32), jax.ShapeDtypeStruct((M, LANES), F32)],
        compiler_params=_cparams(("parallel",)),
        name="route",
    )(logits, bias.reshape(1, E))


def _concat_rows_kernel(a_ref, b_ref, o_ref, *, na_blocks):
    i = pl.program_id(0)

    @pl.when(i < na_blocks)
    def _():
        o_ref[...] = a_ref[...]

    @pl.when(i >= na_blocks)
    def _():
        o_ref[...] = b_ref[...]


def _concat_rows(a, b, tm=256):
    na, _, H = a.shape
    nb = b.shape[0]
    nab, nbb = na // tm, nb // tm
    return pl.pallas_call(
        functools.partial(_concat_rows_kernel, na_blocks=nab),
        grid=(nab + nbb,),
        in_specs=[pl.BlockSpec((tm, None, H), lambda i: (jnp.minimum(i, nab - 1), 0, 0)),
                  pl.BlockSpec((tm, None, H), lambda i: (jnp.maximum(i - nab, 0), 0, 0))],
        out_specs=pl.BlockSpec((tm, None, H), lambda i: (i, 0, 0)),
        out_shape=jax.ShapeDtypeStruct((na + nb, 1, H), a.dtype),
        compiler_params=_cparams(("arbitrary",)),
        name="concat_rows",
    )(a, b)


def _moe_kernel(ce_ref, nxt_ref, meta_ref, srcn_ref, srcc_ref, dst_ref, x_hbm, wg_hbm, wu_hbm,
                wd_hbm, y_hbm, xbuf, ybuf, zbuf, xbb, wgs, wus, wds, wgb, wub, wdb, src_s, dst_s,
                gsem, ssem, isem, wsem):
    c = pl.program_id(0)
    n_used = meta_ref[0]
    n_steps = pl.num_programs(0)
    slot = c % 2
    R = MOE_ROWS

    def start_gather(tok, s, i, u):
        pltpu.async_copy(x_hbm.at[tok], xbuf.at[s, i, pl.ds(u, 1), :], gsem.at[s], priority=u % 2)

    def start_scatter(row, s, i, u):
        pltpu.async_copy(ybuf.at[s, i, pl.ds(u, 1), :], y_hbm.at[pl.ds(row, 1), :], ssem.at[s],
                         priority=u % 2)

    def whole_buffer_wait(buf, sem, s):
        pltpu.make_async_copy(buf.at[1 - s], buf.at[s], sem.at[s]).wait()

    def weight_copies(e):
        return [pltpu.make_async_copy(wg_hbm.at[e], wgs, wsem.at[0]),
                pltpu.make_async_copy(wu_hbm.at[e], wus, wsem.at[1]),
                pltpu.make_async_copy(wd_hbm.at[e], wds, wsem.at[2])]

    def src_copy(vmem_ref):
        return pltpu.make_async_copy(vmem_ref, src_s, isem.at[0])

    def dst_copy():
        return pltpu.make_async_copy(dst_ref, dst_s, isem.at[1])

    def for_rows(fn):
        def body(i, carry):
            for u in range(SUBLANES):
                fn(i, u)
            return carry
        lax.fori_loop(0, R // SUBLANES, body, 0)

    def issue_gather(s):
        for_rows(lambda i, u: start_gather(src_s[0, 0, i * SUBLANES + u], s, i, u))

    @pl.when(c == 0)
    def _():
        for cp in weight_copies(ce_ref[0]):
            cp.start()
        cp = src_copy(srcc_ref)
        cp.start()
        cp.wait()
        issue_gather(0)

    src_copy(srcn_ref).start()
    dst_copy().start()

    @pl.when(c >= n_used)
    def _():
        src_copy(srcn_ref).wait()
        dst_copy().wait()

    @pl.when(c < n_used)
    def _():
        prev = jnp.maximum(c - 1, 0)

        @pl.when((c == 0) | (ce_ref[c] != ce_ref[prev]))
        def _():
            for cp in weight_copies(ce_ref[c]):
                cp.wait()
            for stage, dest in ((wgs, wgb), (wus, wub), (wds, wdb)):
                rows = stage.shape[0] // 8
                for r in range(0, stage.shape[0], rows):
                    dest[r:r + rows, :] = stage[r:r + rows, :].astype(BF16)

            @pl.when(nxt_ref[c] >= 0)
            def _():
                for cp in weight_copies(nxt_ref[c]):
                    cp.start()

        whole_buffer_wait(xbuf, gsem, slot)

        @pl.when(c >= 2)
        def _():
            whole_buffer_wait(ybuf, ssem, slot)

        H = xbuf.shape[-1]
        lo, hi = _unpack_pairs(xbuf[slot].reshape(R, H))
        xbb[:, :H] = lo.astype(BF16)
        xbb[:, H:] = hi.astype(BF16)

        src_copy(srcn_ref).wait()
        dst_copy().wait()

        @pl.when(c + 1 < n_used)
        def _():
            issue_gather(1 - slot)

        xb = xbb[...]
        g = jnp.dot(xb, wgb[...], preferred_element_type=F32)
        u = jnp.dot(xb, wub[...], preferred_element_type=F32)
        y = jnp.dot((_silu(g) * u).astype(BF16), wdb[...], preferred_element_type=F32)
        ybuf[slot] = _pack_pairs(y).reshape(R // SUBLANES, SUBLANES, H)
        for_rows(lambda i, u_: start_scatter(dst_s[0, 0, i * SUBLANES + u_], slot, i, u_))

    @pl.when(c == n_steps - 1)
    def _():
        last = n_used - 1
        whole_buffer_wait(ybuf, ssem, last % 2)

        @pl.when(n_used >= 2)
        def _():
            whole_buffer_wait(ybuf, ssem, (last - 1) % 2)

        zbuf[...] = jnp.zeros(zbuf.shape, U32)
        dump0 = y_hbm.shape[0] - 2 * R
        for s in range(2):
            cp = pltpu.make_async_copy(zbuf, y_hbm.at[pl.ds(dump0 + s * R, R), :], ssem.at[0])
            cp.start()
            cp.wait()


def _moe(h2p, idx, w_gate_e, w_up_e, w_down_e):
    M, _, H = h2p.shape
    E, D, De = w_gate_e.shape
    assert D == 2 * H
    R = MOE_ROWS
    A = M * TOP_K
    nch = A // R + E
    flat_e = idx.reshape(A)
    order = jnp.argsort(flat_e, stable=True).astype(I32)
    experts = jnp.arange(E, dtype=I32)
    counts = jnp.sum((flat_e[None, :] == experts[:, None]).astype(I32), axis=1)
    start = jnp.cumsum(counts).astype(I32) - counts
    nch_e = (counts + R - 1) // R
    ch_end = jnp.cumsum(nch_e).astype(I32)
    ch_start = ch_end - nch_e
    n_used = ch_end[-1]
    cid = jnp.arange(nch, dtype=I32)
    ce = jnp.sum((ch_end[None, :] <= jnp.minimum(cid, n_used - 1)[:, None]).astype(I32), axis=1)
    ce = jnp.minimum(ce, E - 1)
    after = ch_end[ce]
    nxt = jnp.where(after < n_used, ce[jnp.minimum(after, nch - 1)], -1).astype(I32)
    rr = jnp.arange(R, dtype=I32)[None, :]
    local = (cid - ch_start[ce])[:, None] * R + rr
    valid = (cid[:, None] < n_used) & (local < counts[ce][:, None])
    a = order[jnp.clip(start[ce][:, None] + local, 0, A - 1)]
    src = jnp.where(valid, a // TOP_K, 0)
    dst = jnp.where(valid, (a % TOP_K) * M + a // TOP_K, TOP_K * M + (cid[:, None] % 2) * R + rr)
    meta = jnp.stack([n_used, n_used]).astype(I32)

    idx_spec = lambda f: pl.BlockSpec((1, 1, R), f)
    hbm = pl.BlockSpec(memory_space=pl.ANY)
    return pl.pallas_call(
        _moe_kernel,
        grid_spec=pltpu.PrefetchScalarGridSpec(
            num_scalar_prefetch=3,
            grid=(nch,),
            in_specs=[idx_spec(lambda c, *_: (jnp.minimum(c + 1, nch - 1), 0, 0)),
                      idx_spec(lambda c, *_: (c, 0, 0)),
                      idx_spec(lambda c, *_: (c, 0, 0)),
                      hbm, hbm, hbm, hbm],
            out_specs=hbm,
            scratch_shapes=[pltpu.VMEM((2, R // SUBLANES, SUBLANES, H), U32),
                            pltpu.VMEM((2, R // SUBLANES, SUBLANES, H), U32), pltpu.VMEM((R, H), U32),
                            pltpu.VMEM((R, D), BF16), pltpu.VMEM((D, De), F32), pltpu.VMEM((D, De), F32), pltpu.VMEM((De, D), F32),
                            pltpu.VMEM((D, De), BF16), pltpu.VMEM((D, De), BF16),
                            pltpu.VMEM((De, D), BF16),
                            pltpu.SMEM((1, 1, R), I32), pltpu.SMEM((1, 1, R), I32),
                            pltpu.SemaphoreType.DMA((2,)), pltpu.SemaphoreType.DMA((2,)),
                            pltpu.SemaphoreType.DMA((2,)), pltpu.SemaphoreType.DMA((3,))]),
        out_shape=jax.ShapeDtypeStruct((TOP_K * M + 2 * R, H), U32),
        compiler_params=_cparams(("arbitrary",)),
        name="moe",
    )(ce, nxt, meta, src.reshape(nch, 1, R), src.reshape(nch, 1, R), dst.reshape(nch, 1, R),
      h2p, w_gate_e, w_up_e, w_down_e)


def _combine_kernel(base_ref, g2_ref, gw_ref, *refs):
    o_ref = refs[-1]
    H = o_ref.shape[-1] // 2
    gates = [gw_ref[:, k:k + 1] for k in range(TOP_K)]
    for j in range(0, H, LANES):
        halves = [_unpack_pairs(r[:, j:j + LANES]) for r in refs[:-1]]
        for part, sl in ((0, slice(j, j + LANES)), (1, slice(H + j, H + j + LANES))):
            routed = _tree_sum([h[part] * w for h, w in zip(halves, gates)])
            o_ref[:, sl] = base_ref[:, sl] + g2_ref[:, sl] * routed


def _combine(base, g2, y, gw, *, m_total, tok0, tm, per_token):
    Bx, T, D = base.shape
    nt = T // tm

    def plane(kk):
        off = (kk * m_total + tok0) // tm
        return pl.BlockSpec((tm, D // 2), lambda b, t: (off + b * nt + t, 0))

    return pl.pallas_call(
        _combine_kernel,
        grid=(Bx, nt),
        in_specs=[pl.BlockSpec((None, tm, D), lambda b, t: (b, t, 0)), _mod_spec(per_token, tm, D),
                  pl.BlockSpec((tm, LANES), lambda b, t: (tok0 // tm + b * nt + t, 0))]
                 + [plane(kk) for kk in range(TOP_K)],
        out_specs=pl.BlockSpec((None, tm, D), lambda b, t: (b, t, 0)),
        out_shape=jax.ShapeDtypeStruct((Bx, T, D), F32),
        compiler_params=_cparams(("parallel", "arbitrary")),
        name="combine",
    )(base, g2, gw, *([y] * TOP_K))


def _repack_w_in(w_in, b_in):
    c_ki = 3 * D_ATTN + IDX_HEADS * IDX_DIM
    c_wi = c_ki + IDX_DIM
    c_uv = c_wi + IDX_HEADS

    def pack(a):
        z = lambda n: jnp.zeros(a.shape[:-1] + (n,), a.dtype)
        return jnp.concatenate([a[..., :c_wi], z(C_WI - C_KI - IDX_DIM), a[..., c_wi:c_uv],
                                z(C_UV - C_WI - IDX_HEADS), a[..., c_uv:]], axis=-1)

    return pack(w_in).astype(BF16), pack(b_in[None, :])


def _pad_rows(a, n):
    return jnp.pad(a, ((0, 0), (0, n - a.shape[1]), (0, 0)))


def kernel(x_prompt, x_sample, cache_k, cache_v, cache_kidx, state_conv, c_prompt, c_sample, norm1_g, norm2_g, w_ada, b_ada, w_in, b_in, q_norm_g, k_norm_g, idx_k_ln_g, idx_k_ln_b, dw_w, dw_b, conv_ln_g, conv_ln_b, w_out, b_out, w_router, router_bias, w_gate_e, w_up_e, w_down_e, w_sh_gate, w_sh_up, w_sh_down):
    B, T, D = x_prompt.shape
    Bd, Td, _ = x_sample.shape
    depth = w_in.shape[0]
    past = cache_k.shape[2]
    Ms, Mp = Bd * Td, B * T
    M = Mp + Ms
    C = D - D_ATTN
    topk_p = min(IDX_TOPK_MAX, T // 4)
    topk_s = min(IDX_TOPK_MAX, (past + Td) // 4)
    l_s = past + Td
    s_pad = -(-l_s // KC) * KC
    tm_p, tm_s = 256, 128
    xp, xs = x_prompt, x_sample.reshape(1, Ms, D)
    outs = [[] for _ in range(8)]
    for l in range(depth):
        mods = _ada(jnp.concatenate([c_prompt, c_sample], axis=0), w_ada[l], b_ada[l])
        mp = [m[:, None, :] for m in jnp.split(mods[:B], 6, axis=-1)]
        ms = [jnp.repeat(m, Td, axis=0)[None] for m in jnp.split(mods[B:], 6, axis=-1)]
        w2, b2 = _repack_w_in(w_in[l], b_in[l])
        row = lambda a: a.reshape(1, -1)
        proj_w = (row(norm1_g[l]), w2, b2, row(q_norm_g[l]), row(k_norm_g[l]),
                  row(idx_k_ln_g[l]), row(idx_k_ln_b[l]))
        woa, wob = w_out[l, :D_ATTN].astype(BF16), w_out[l, D_ATTN:].astype(BF16)
        out_w = (row(norm2_g[l]), woa, wob, row(b_out[l]), w_router[l].astype(BF16),
                 w_sh_gate[l].astype(BF16), w_sh_up[l].astype(BF16), w_sh_down[l].astype(BF16))

        q, k, kb, v, vb, qi, ki, kib, wi, glu = _inproj(xp, mp[0], mp[1], *proj_w, tm=tm_p, per_token=False)
        attn = _dsa(q, qi, wi, kb, vb, kib, q_base=0, l_valid=T, topk=topk_p, causal=True)
        conv = _conv(glu, jnp.zeros((B, CTX_PAD, C), F32), dw_w[l], dw_b[l], conv_ln_g[l], conv_ln_b[l])
        base_p, h2_p, lg_p = _outproj(attn, conv, xp, mp[2], mp[3], mp[4], mp[5], *out_w,
                                      tm=tm_p, per_token=False)
        outs[0].append(k.reshape(B, T, N_HEADS, HEAD_DIM))
        outs[1].append(v.reshape(B, T, N_HEADS, HEAD_DIM))
        outs[2].append(ki)
        outs[3].append(glu[:, T - CTX:])

        q, k, kb, v, vb, qi, ki, kib, wi, glu = _inproj(xs, ms[0], ms[1], *proj_w, tm=tm_s, per_token=True)
        per_b = lambda a: a.reshape(Bd, Td, a.shape[-1])
        cat = lambda cache, new: _pad_rows(
            jnp.concatenate([cache.reshape(Bd, past, -1).astype(BF16), per_b(new)], axis=1), s_pad)
        attn = _dsa(_pad_rows(per_b(q), QB), _pad_rows(per_b(qi), QB), _pad_rows(per_b(wi), QB),
                    cat(cache_k[l], kb), cat(cache_v[l], vb), cat(cache_kidx[l], kib),
                    q_base=past, l_valid=l_s, topk=topk_s, causal=False)
        attn = attn[:, :Td].reshape(1, Ms, D_ATTN)
        ext = jnp.concatenate([state_conv[l], per_b(glu)], axis=1)
        ctx = jnp.pad(state_conv[l], ((0, 0), (CTX_PAD - CTX, 0), (0, 0)))
        conv = _conv(per_b(glu), ctx, dw_w[l], dw_b[l], conv_ln_g[l], conv_ln_b[l]).reshape(1, Ms, C)
        base_s, h2_s, lg_s = _outproj(attn, conv, xs, ms[2], ms[3], ms[4], ms[5], *out_w,
                                      tm=tm_s, per_token=True)
        outs[4].append(per_b(k).reshape(Bd, Td, N_HEADS, HEAD_DIM))
        outs[5].append(per_b(v).reshape(Bd, Td, N_HEADS, HEAD_DIM))
        outs[6].append(per_b(ki))
        outs[7].append(ext[:, Td:])

        h2 = _concat_rows(h2_p.reshape(Mp, 1, D // 2), h2_s.reshape(Ms, 1, D // 2))
        lg = jnp.concatenate([lg_p.reshape(Mp, -1), lg_s.reshape(Ms, -1)], axis=0)
        idx, gw = _route(lg, router_bias[l])
        y = _moe(h2, idx[:, :TOP_K], w_gate_e[l], w_up_e[l], w_down_e[l])
        xp = _combine(base_p, mp[5], y, gw, m_total=M, tok0=0, tm=tm_p, per_token=False)
        xs = _combine(base_s, ms[5], y, gw, m_total=M, tok0=Mp, tm=256, per_token=True)
    return (xp, xs.reshape(Bd, Td, D), *[jnp.stack(o) for o in outs])
```

```python
import functools

import numpy as np
import jax
import jax.numpy as jnp
from jax import lax
from jax.experimental import pallas as pl
from jax.experimental.pallas import tpu as pltpu

F32, BF16, I32, U32 = jnp.float32, jnp.bfloat16, jnp.int32, jnp.uint32

N_HEADS = 8
HEAD_DIM = 128
D_ATTN = N_HEADS * HEAD_DIM
IDX_HEADS = 16
IDX_DIM = 64
CHUNK = 64
CHUNK_SHIFT = 6
CONV_WIDTH = 31
CTX = CONV_WIDTH - 1
IDX_TOPK_MAX = 256
N_GROUPS = 8
TOPK_GROUPS = 4
TOP_K = 8
ROUTED_SCALE = 2.5
EPS = 1e-6
NEG_INF = -1e30

LANES = 128
SUBLANES = 8
VMEM_LIMIT = 56 * 1024 * 1024

QB = 128
KC = 256
CTX_PAD = 32
MOE_ROWS = 256
ROW_TILES = 16


def _cparams(sem):
    return pltpu.CompilerParams(dimension_semantics=sem, vmem_limit_bytes=VMEM_LIMIT)


def _const_spec(shape):
    nd = len(shape)
    return pl.BlockSpec(shape, lambda *_: (0,) * nd, pipeline_mode=pl.Buffered(1))


def _silu(x):
    return x * jax.nn.sigmoid(x)


def _pack_pairs(x):
    h = x.shape[-1] // 2
    bits = lax.bitcast_convert_type(x.astype(BF16).astype(F32), U32)
    return (bits[:, h:] & jnp.uint32(0xFFFF0000)) | (bits[:, :h] >> 16)


def _unpack_pairs(p):
    lo = lax.bitcast_convert_type(p << 16, F32)
    hi = lax.bitcast_convert_type(p & jnp.uint32(0xFFFF0000), F32)
    return lo, hi


def _tree_sum(parts):
    while len(parts) > 1:
        parts = [a + b for a, b in zip(parts[::2], parts[1::2])] + parts[len(parts) & ~1:]
    return parts[0]


def _ada_kernel(c_ref, w_ref, b_ref, o_ref):
    a = _silu(c_ref[...]).astype(BF16)
    o_ref[...] = jnp.dot(a, w_ref[...].astype(BF16), preferred_element_type=F32) + b_ref[...]


def _ada(c_all, w_ada, b_ada):
    R, D = c_all.shape
    N = w_ada.shape[1]
    TN = 1024
    return pl.pallas_call(
        _ada_kernel,
        grid=(N // TN,),
        in_specs=[pl.BlockSpec((R, D), lambda n: (0, 0)),
                  pl.BlockSpec((D, TN), lambda n: (0, n)),
                  pl.BlockSpec((1, TN), lambda n: (0, n))],
        out_specs=pl.BlockSpec((R, TN), lambda n: (0, n)),
        out_shape=jax.ShapeDtypeStruct((R, N), F32),
        compiler_params=_cparams(("arbitrary",)),
        name="ada",
    )(c_all, w_ada, b_ada.reshape(1, N))


C_Q, C_K, C_V, C_QI = 0, 1024, 2048, 3072
C_KI, C_WI, C_UV, C_UG, C_END = 4096, 4224, 4352, 5376, 6400


def _inproj_kernel(x_ref, sh_ref, sc_ref, g_ref, w_ref, b_ref, qg_ref, kg_ref, lg_ref, lb_ref,
                   q_ref, k_ref, kb_ref, v_ref, vb_ref, qi_ref, ki_ref, kib_ref, wi_ref, glu_ref):
    x = x_ref[...]
    y = x * lax.rsqrt(jnp.mean(x * x, axis=-1, keepdims=True) + EPS) * g_ref[...]
    hb = (y * (1.0 + sc_ref[...]) + sh_ref[...]).astype(BF16)

    def proj(lo, hi):
        return jnp.dot(hb, w_ref[:, lo:hi], preferred_element_type=F32) + b_ref[:, lo:hi]

    def head_norm(z, gain):
        return z * lax.rsqrt(jnp.mean(z * z, axis=-1, keepdims=True) + EPS) * gain

    zq = proj(C_Q, C_K)
    zk = proj(C_K, C_V)
    for h in range(N_HEADS):
        sl = slice(h * HEAD_DIM, (h + 1) * HEAD_DIM)
        q_ref[:, sl] = head_norm(zq[:, sl], qg_ref[...]).astype(BF16)
        kn = head_norm(zk[:, sl], kg_ref[...])
        k_ref[:, sl] = kn
        kb_ref[:, sl] = kn.astype(BF16)
    zv = proj(C_V, C_QI)
    v_ref[...] = zv
    vb_ref[...] = zv.astype(BF16)
    qi_ref[...] = (proj(C_QI, C_KI) * (IDX_DIM ** -0.5)).astype(BF16)
    zki = proj(C_KI, C_WI)[:, :IDX_DIM]
    mu = jnp.mean(zki, axis=-1, keepdims=True)
    xc = zki - mu
    var = jnp.mean(xc * xc, axis=-1, keepdims=True)
    ki = xc * lax.rsqrt(var + EPS) * lg_ref[...] + lb_ref[...]
    ki_ref[...] = ki
    kib_ref[...] = ki.astype(BF16)
    wi_ref[...] = proj(C_WI, C_UV)[:, :IDX_HEADS] * (IDX_HEADS ** -0.5)
    glu_ref[...] = proj(C_UV, C_UG) * jax.nn.sigmoid(proj(C_UG, C_END))


def _mod_spec(per_token, tm, d):
    if per_token:
        return pl.BlockSpec((None, tm, d), lambda b, t: (b, t, 0))
    return pl.BlockSpec((None, 1, d), lambda b, t: (b, 0, 0))


def _inproj(x, sh, sc, g, w2, b2, qg, kg, lg, lb, *, tm, per_token):
    Bx, T, D = x.shape
    tok = lambda n: pl.BlockSpec((None, tm, n), lambda b, t: (b, t, 0))
    shp = lambda n, dt: jax.ShapeDtypeStruct((Bx, T, n), dt)
    return pl.pallas_call(
        _inproj_kernel,
        grid=(Bx, T // tm),
        in_specs=[tok(D), _mod_spec(per_token, tm, D), _mod_spec(per_token, tm, D),
                  _const_spec((1, D)), _const_spec((D, C_END)), _const_spec((1, C_END)),
                  _const_spec((1, HEAD_DIM)), _const_spec((1, HEAD_DIM)),
                  _const_spec((1, IDX_DIM)), _const_spec((1, IDX_DIM))],
        out_specs=[tok(D_ATTN), tok(D_ATTN), tok(D_ATTN), tok(D_ATTN), tok(D_ATTN),
                   tok(IDX_HEADS * IDX_DIM), tok(IDX_DIM), tok(IDX_DIM), tok(IDX_HEADS),
                   tok(D - D_ATTN)],
        out_shape=[shp(D_ATTN, BF16), shp(D_ATTN, F32), shp(D_ATTN, BF16), shp(D_ATTN, F32),
                   shp(D_ATTN, BF16), shp(IDX_HEADS * IDX_DIM, BF16), shp(IDX_DIM, F32),
                   shp(IDX_DIM, BF16), shp(IDX_HEADS, F32), shp(D - D_ATTN, F32)],
        compiler_params=_cparams(("parallel", "arbitrary")),
        name="inproj",
    )(x, sh, sc, g, w2, b2, qg, kg, lg, lb)


def _alibi_slopes():
    return [float(np.float32(2.0) ** np.float32(-8.0 * h / N_HEADS)) for h in range(1, N_HEADS + 1)]


def _dsa_kernel(qT_ref, qiw_ref, wiw_ref, k_ref, vT_ref, ki_ref, o_ref,
                key_ref, sel_ref, acc_ref, m_ref, l_ref, *, q_base, l_valid, topk, causal, n_kc):
    jb = pl.program_id(1)
    nkc = (jb * QB + QB + KC - 1) // KC if causal else n_kc
    lane = lax.broadcasted_iota(I32, (1, QB), 1)
    qpos = q_base + jb * QB + lane
    qchunk = qpos >> CHUNK_SHIFT
    sub = lax.broadcasted_iota(I32, (KC, 1), 0)

    def visible(c):
        kpos = c * KC + sub
        return ((kpos >> CHUNK_SHIFT) <= qchunk) & (kpos < l_valid)

    def score_chunk(c, carry):
        kic = ki_ref[c]
        acc = jnp.zeros((KC, QB), F32)
        for hp in range(IDX_HEADS // 2):
            sl = slice(hp * 2 * QB, (hp + 1) * 2 * QB)
            r = jnp.dot(kic, qiw_ref[:, sl], preferred_element_type=F32)
            t = jnp.maximum(r, 0.0) * wiw_ref[:, sl]
            acc = acc + t[:, :QB] + t[:, QB:]
        bits = pltpu.bitcast(jnp.where(visible(c), acc, -jnp.inf), I32)
        key_ref[c] = bits ^ ((bits >> 31) & 0x7FFFFFFF)
        return carry

    lax.fori_loop(0, nkc, score_chunk, 0)

    def count(pred):
        def body(c, cnt):
            hit = jnp.where(pred(key_ref[c]), 1.0, 0.0)
            return cnt + _tree_sum([hit[i:i + SUBLANES] for i in range(0, KC, SUBLANES)])
        cnt = lax.fori_loop(0, nkc, body, jnp.zeros((SUBLANES, QB), F32))
        return jnp.sum(cnt, axis=0, keepdims=True)

    kf = float(topk)
    int_min = jnp.full((1, QB), -2 ** 31, I32)
    zero = jnp.zeros((1, QB), I32)
    thr0 = jnp.where(count(lambda key: key >= zero) >= kf, zero, int_min)

    def bisect(i, lo):
        cand = lo | jnp.left_shift(jnp.int32(1), 30 - i)
        return jnp.where(count(lambda key: key >= cand) >= kf, cand, lo)

    thr = lax.fori_loop(0, 31, bisect, thr0)
    need = kf - count(lambda key: key > thr)

    tri = (lax.broadcasted_iota(I32, (KC, KC), 0) >= lax.broadcasted_iota(I32, (KC, KC), 1))
    tri = jnp.where(tri, 1.0, 0.0).astype(BF16)

    def select_chunk(c, carry):
        key = key_ref[c]
        tie = key == thr
        tie_f = jnp.where(tie, 1.0, 0.0)
        pref = jnp.dot(tri, tie_f.astype(BF16), preferred_element_type=F32) + carry
        sel = ((key > thr) | (tie & (pref <= need))) & visible(c)
        sel_ref[c] = jnp.where(sel, 1.0, 0.0)
        return carry + jnp.sum(tie_f, axis=0, keepdims=True)

    lax.fori_loop(0, nkc, select_chunk, jnp.zeros((1, QB), F32))

    m_ref[...] = jnp.full((N_HEADS, QB), 0.1 * NEG_INF, F32)
    l_ref[...] = jnp.zeros((N_HEADS, QB), F32)
    acc_ref[...] = jnp.zeros((D_ATTN, QB), F32)
    scale = HEAD_DIM ** -0.5
    slopes = [s / scale for s in _alibi_slopes()]
    c2 = scale * float(np.log2(np.e))

    def attend_chunk(c, carry):
        dist = jnp.abs(qpos - (c * KC + sub)).astype(F32)
        selc = sel_ref[c] > 0.5
        for h in range(N_HEADS):
            hs = slice(h * HEAD_DIM, (h + 1) * HEAD_DIM)
            lt = jnp.dot(k_ref[c, :, hs], qT_ref[hs, :], preferred_element_type=F32)
            lt = jnp.where(selc, lt - slopes[h] * dist, NEG_INF)
            m_old = m_ref[h:h + 1, :]
            m_new = jnp.maximum(m_old, jnp.max(lt, axis=0, keepdims=True))
            alpha = jnp.exp2((m_old - m_new) * c2)
            p = jnp.exp2((lt - m_new) * c2)
            l_ref[h:h + 1, :] = alpha * l_ref[h:h + 1, :] + jnp.sum(p, axis=0, keepdims=True)
            pv = jnp.dot(vT_ref[c, hs, :], p.astype(BF16), preferred_element_type=F32)
            acc_ref[hs, :] = acc_ref[hs, :] * alpha + pv
            m_ref[h:h + 1, :] = m_new
        return carry

    lax.fori_loop(0, nkc, attend_chunk, 0)

    for h in range(N_HEADS):
        hs = slice(h * HEAD_DIM, (h + 1) * HEAD_DIM)
        o = acc_ref[hs, :] / l_ref[h:h + 1, :]
        o_ref[:, hs] = o.T.astype(BF16)


def _dsa(q, qi, wi, k, v, ki, *, q_base, l_valid, topk, causal):
    B, Tq, _ = q.shape
    S = k.shape[1]
    nj, n_kc = Tq // QB, S // KC
    qT = q.transpose(0, 2, 1)
    qiw = qi.reshape(B, nj, QB, IDX_HEADS, IDX_DIM).transpose(0, 1, 4, 3, 2)
    qiw = qiw.reshape(B, nj, IDX_DIM, IDX_HEADS * QB)
    wiw = wi.reshape(B, nj, QB, IDX_HEADS).transpose(0, 1, 3, 2).reshape(B, nj, 1, IDX_HEADS * QB)
    kc = k.reshape(B, n_kc, KC, D_ATTN)
    vT = v.reshape(B, n_kc, KC, D_ATTN).transpose(0, 1, 3, 2)
    kic = ki.reshape(B, n_kc, KC, IDX_DIM)
    kern = functools.partial(_dsa_kernel, q_base=q_base, l_valid=l_valid, topk=topk,
                             causal=causal, n_kc=n_kc)
    return pl.pallas_call(
        kern,
        grid=(B, nj),
        in_specs=[pl.BlockSpec((None, D_ATTN, QB), lambda b, j: (b, 0, j)),
                  pl.BlockSpec((None, None, IDX_DIM, IDX_HEADS * QB), lambda b, j: (b, j, 0, 0)),
                  pl.BlockSpec((None, None, 1, IDX_HEADS * QB), lambda b, j: (b, j, 0, 0)),
                  pl.BlockSpec((None, n_kc, KC, D_ATTN), lambda b, j: (b, 0, 0, 0)),
                  pl.BlockSpec((None, n_kc, D_ATTN, KC), lambda b, j: (b, 0, 0, 0)),
                  pl.BlockSpec((None, n_kc, KC, IDX_DIM), lambda b, j: (b, 0, 0, 0))],
        out_specs=pl.BlockSpec((None, QB, D_ATTN), lambda b, j: (b, j, 0)),
        out_shape=jax.ShapeDtypeStruct((B, Tq, D_ATTN), BF16),
        scratch_shapes=[pltpu.VMEM((n_kc, KC, QB), I32), pltpu.VMEM((n_kc, KC, QB), F32),
                        pltpu.VMEM((D_ATTN, QB), F32), pltpu.VMEM((N_HEADS, QB), F32),
                        pltpu.VMEM((N_HEADS, QB), F32)],
        compiler_params=_cparams(("parallel", "arbitrary")),
        name="dsa",
    )(qT, qiw, wiw, kc, vT, kic)


def _conv_kernel(glu_ref, ctx_ref, w_ref, b_ref, g_ref, be_ref, o_ref, ext_ref, *, T, rt):
    C = glu_ref.shape[-1]
    ext_ref[0:CTX_PAD, :] = ctx_ref[...]
    ext_ref[CTX_PAD:CTX_PAD + T, :] = glu_ref[...]
    ext_ref[CTX_PAD + T:, :] = jnp.zeros((SUBLANES, C), F32)
    first = CTX_PAD - CTX

    def tile(i, carry):
        t0 = pl.multiple_of(i * rt, rt)
        win = ext_ref[pl.ds(t0, rt + CTX_PAD + SUBLANES), :]
        nwin = rt + CTX_PAD + SUBLANES
        shifted = [win if r == 0 else pltpu.roll(win, nwin - r, axis=0) for r in range(SUBLANES)]
        acc = jnp.zeros((rt, C), F32) + b_ref[...]
        for kk in range(CONV_WIDTH):
            a, r = divmod(first + kk, SUBLANES)
            acc = acc + w_ref[kk:kk + 1, :] * shifted[r][SUBLANES * a:SUBLANES * a + rt]
        mu = jnp.mean(acc, axis=-1, keepdims=True)
        xc = acc - mu
        var = jnp.mean(xc * xc, axis=-1, keepdims=True)
        y = xc * lax.rsqrt(var + EPS) * g_ref[...] + be_ref[...]
        o_ref[pl.ds(t0, rt), :] = _silu(y).astype(BF16)
        return carry

    lax.fori_loop(0, T // rt, tile, 0)


def _conv(glu, ctx, dw_w, dw_b, ln_g, ln_b):
    B, T, C = glu.shape
    rt = min(64, T)
    wpad = jnp.pad(dw_w, ((0, CTX_PAD - CONV_WIDTH), (0, 0)))
    kern = functools.partial(_conv_kernel, T=T, rt=rt)
    return pl.pallas_call(
        kern,
        grid=(B,),
        in_specs=[pl.BlockSpec((None, T, C), lambda b: (b, 0, 0)),
                  pl.BlockSpec((None, CTX_PAD, C), lambda b: (b, 0, 0)),
                  _const_spec((CTX_PAD, C)), _const_spec((1, C)), _const_spec((1, C)),
                  _const_spec((1, C))],
        out_specs=pl.BlockSpec((None, T, C), lambda b: (b, 0, 0)),
        out_shape=jax.ShapeDtypeStruct((B, T, C), BF16),
        scratch_shapes=[pltpu.VMEM((T + CTX_PAD + SUBLANES, C), F32)],
        compiler_params=_cparams(("parallel",)),
        name="conv",
    )(glu, ctx, wpad, dw_b.reshape(1, C), ln_g.reshape(1, C), ln_b.reshape(1, C))


def _outproj_kernel(attn_ref, conv_ref, x_ref, g1_ref, sh2_ref, sc2_ref, g2_ref, n2_ref,
                    woa_ref, wob_ref, bo_ref, wr_ref, wsg_ref, wsu_ref, wsd_ref,
                    base_ref, h2_ref, lg_ref):
    mix = (jnp.dot(attn_ref[...], woa_ref[...], preferred_element_type=F32)
           + jnp.dot(conv_ref[...], wob_ref[...], preferred_element_type=F32) + bo_ref[...])
    x1 = x_ref[...] + g1_ref[...] * mix
    y = x1 * lax.rsqrt(jnp.mean(x1 * x1, axis=-1, keepdims=True) + EPS) * n2_ref[...]
    h2 = y * (1.0 + sc2_ref[...]) + sh2_ref[...]
    h2_ref[...] = _pack_pairs(h2)
    hb = h2.astype(BF16)
    lg_ref[...] = jnp.dot(hb, wr_ref[...], preferred_element_type=F32)
    sg = jnp.dot(hb, wsg_ref[...], preferred_element_type=F32)
    su = jnp.dot(hb, wsu_ref[...], preferred_element_type=F32)
    shared = jnp.dot((_silu(sg) * su).astype(BF16), wsd_ref[...], preferred_element_type=F32)
    base_ref[...] = x1 + g2_ref[...] * shared


def _outproj(attn, conv, x, g1, sh2, sc2, g2, n2, woa, wob, bo, wr, wsg, wsu, wsd, *, tm, per_token):
    Bx, T, D = x.shape
    E = wr.shape[1]
    Ds = wsg.shape[1]
    tok = lambda n: pl.BlockSpec((None, tm, n), lambda b, t: (b, t, 0))
    mod = lambda: _mod_spec(per_token, tm, D)
    return pl.pallas_call(
        _outproj_kernel,
        grid=(Bx, T // tm),
        in_specs=[tok(D_ATTN), tok(D - D_ATTN), tok(D), mod(), mod(), mod(), mod(),
                  _const_spec((1, D)), _const_spec((D_ATTN, D)), _const_spec((D - D_ATTN, D)),
                  _const_spec((1, D)), _const_spec((D, E)), _const_spec((D, Ds)),
                  _const_spec((D, Ds)), _const_spec((Ds, D))],
        out_specs=[tok(D), pl.BlockSpec((None, tm, None, D // 2), lambda b, t: (b, t, 0, 0)), tok(E)],
        out_shape=[jax.ShapeDtypeStruct((Bx, T, D), F32), jax.ShapeDtypeStruct((Bx, T, 1, D // 2), U32),
                   jax.ShapeDtypeStruct((Bx, T, E), F32)],
        compiler_params=_cparams(("parallel", "arbitrary")),
        name="outproj",
    )(attn, conv, x, g1, sh2, sc2, g2, n2, woa, wob, bo, wr, wsg, wsu, wsd)


def _route_kernel(lg_ref, bias_ref, idx_ref, w_ref):
    tm, E = lg_ref.shape
    gsz = E // N_GROUPS
    scores = jax.nn.sigmoid(lg_ref[...])
    biased = scores + bias_ref[...]
    lane = lax.broadcasted_iota(I32, (tm, E), 1).astype(F32)
    lane_i = lax.broadcasted_iota(I32, (tm, E), 1)
    in_group = lambda g: (lane_i >= g * gsz) & (lane_i < (g + 1) * gsz)
    ninf = -jnp.inf

    def first_argmax(vals):
        m = jnp.max(vals, axis=-1, keepdims=True)
        i = jnp.min(jnp.where(vals == m, lane, float(E)), axis=-1, keepdims=True)
        return m, i

    gs = []
    for g in range(N_GROUPS):
        mg = jnp.where(in_group(g), biased, ninf)
        m1, i1 = first_argmax(mg)
        m2 = jnp.max(jnp.where(lane == i1, ninf, mg), axis=-1, keepdims=True)
        gs.append(m1 + m2)
    keep = jnp.zeros((tm, E), jnp.bool_)
    for g in range(N_GROUPS):
        rank = jnp.zeros((tm, 1), F32)
        for o in range(N_GROUPS):
            if o == g:
                continue
            beats = (gs[o] > gs[g]) | ((gs[o] == gs[g]) & (o < g))
            rank = rank + jnp.where(beats, 1.0, 0.0)
        keep = keep | (in_group(g) & (rank < float(TOPK_GROUPS)))
    masked = jnp.where(keep, biased, ninf)
    out_lane = lax.broadcasted_iota(I32, (tm, LANES), 1)
    idx_out = jnp.zeros((tm, LANES), F32)
    w_out = jnp.zeros((tm, LANES), F32)
    for kk in range(TOP_K):
        _, i = first_argmax(masked)
        hit = lane == i
        wk = jnp.sum(jnp.where(hit, scores, 0.0), axis=-1, keepdims=True)
        masked = jnp.where(hit, ninf, masked)
        idx_out = jnp.where(out_lane == kk, i, idx_out)
        w_out = jnp.where(out_lane == kk, wk, w_out)
    w_out = w_out / jnp.sum(w_out, axis=-1, keepdims=True) * ROUTED_SCALE
    idx_ref[...] = idx_out.astype(I32)
    w_ref[...] = w_out


def _route(logits, bias):
    M, E = logits.shape
    tm = 256
    return pl.pallas_call(
        _route_kernel,
        grid=(M // tm,),
        in_specs=[pl.BlockSpec((tm, E), lambda i: (i, 0)), _const_spec((1, E))],
        out_specs=[pl.BlockSpec((tm, LANES), lambda i: (i, 0)), pl.BlockSpec((tm, LANES), lambda i: (i, 0))],
        out_shape=[jax.ShapeDtypeStruct((M, LANES), I32), jax.ShapeDtypeStruct((M, LANES), F32)],
        compiler_params=_cparams(("parallel",)),
        name="route",
    )(logits, bias.reshape(1, E))


def _concat_rows_kernel(a_ref, b_ref, o_ref, *, na_blocks):
    i = pl.program_id(0)

    @pl.when(i < na_blocks)
    def _():
        o_ref[...] = a_ref[...]

    @pl.when(i >= na_blocks)
    def _():
        o_ref[...] = b_ref[...]


def _concat_rows(a, b, tm=256):
    na, _, H = a.shape
    nb = b.shape[0]
    nab, nbb = na // tm, nb // tm
    return pl.pallas_call(
        functools.partial(_concat_rows_kernel, na_blocks=nab),
        grid=(nab + nbb,),
        in_specs=[pl.BlockSpec((tm, None, H), lambda i: (jnp.minimum(i, nab - 1), 0, 0)),
                  pl.BlockSpec((tm, None, H), lambda i: (jnp.maximum(i - nab, 0), 0, 0))],
        out_specs=pl.BlockSpec((tm, None, H), lambda i: (i, 0, 0)),
        out_shape=jax.ShapeDtypeStruct((na + nb, 1, H), a.dtype),
        compiler_params=_cparams(("arbitrary",)),
        name="concat_rows",
    )(a, b)


def _moe_kernel(ce_ref, nxt_ref, meta_ref, srcn_ref, srcc_ref, dst_ref, x_hbm, wg_hbm, wu_hbm,
                wd_hbm, y_hbm, xbuf, ybuf, zbuf, xbb, wgs, wus, wds, wgb, wub, wdb, src_s, dst_s,
                gsem, ssem, isem, wsem):
    c = pl.program_id(0)
    n_used = meta_ref[0]
    n_steps = pl.num_programs(0)
    slot = c % 2
    R = MOE_ROWS

    def gather_copy(tok, s, i, u):
        return pltpu.make_async_copy(x_hbm.at[tok], xbuf.at[s, i, pl.ds(u, 1), :], gsem.at[s])

    def scatter_copy(row, s, i, u):
        return pltpu.make_async_copy(ybuf.at[s, i, pl.ds(u, 1), :], y_hbm.at[pl.ds(row, 1), :], ssem.at[s])

    def whole_buffer_wait(buf, sem, s):
        pltpu.make_async_copy(buf.at[1 - s], buf.at[s], sem.at[s]).wait()

    def weight_copies(e):
        return [pltpu.make_async_copy(wg_hbm.at[e], wgs, wsem.at[0]),
                pltpu.make_async_copy(wu_hbm.at[e], wus, wsem.at[1]),
                pltpu.make_async_copy(wd_hbm.at[e], wds, wsem.at[2])]

    def start_weights(e):
        pltpu.async_copy(wg_hbm.at[e], wgs, wsem.at[0], priority=1)
        pltpu.async_copy(wu_hbm.at[e], wus, wsem.at[1], priority=1)
        pltpu.async_copy(wd_hbm.at[e], wds, wsem.at[2], priority=1)

    def src_copy(vmem_ref):
        return pltpu.make_async_copy(vmem_ref, src_s, isem.at[0])

    def dst_copy():
        return pltpu.make_async_copy(dst_ref, dst_s, isem.at[1])

    def for_rows(fn):
        def body(i, carry):
            for u in range(SUBLANES):
                fn(i, u)
            return carry
        lax.fori_loop(0, R // SUBLANES, body, 0)

    def issue_gather(s):
        for_rows(lambda i, u: gather_copy(src_s[0, 0, i * SUBLANES + u], s, i, u).start())

    @pl.when(c == 0)
    def _():
        start_weights(ce_ref[0])
        cp = src_copy(srcc_ref)
        cp.start()
        cp.wait()
        issue_gather(0)

    src_copy(srcn_ref).start()
    dst_copy().start()

    @pl.when(c >= n_used)
    def _():
        src_copy(srcn_ref).wait()
        dst_copy().wait()

    @pl.when(c < n_used)
    def _():
        prev = jnp.maximum(c - 1, 0)

        @pl.when((c == 0) | (ce_ref[c] != ce_ref[prev]))
        def _():
            for cp in weight_copies(ce_ref[c]):
                cp.wait()
            for stage, dest in ((wgs, wgb), (wus, wub), (wds, wdb)):
                rows = stage.shape[0] // 8
                for r in range(0, stage.shape[0], rows):
                    dest[r:r + rows, :] = stage[r:r + rows, :].astype(BF16)

            @pl.when(nxt_ref[c] >= 0)
            def _():
                start_weights(nxt_ref[c])

        whole_buffer_wait(xbuf, gsem, slot)

        @pl.when(c >= 2)
        def _():
            whole_buffer_wait(ybuf, ssem, slot)

        H = xbuf.shape[-1]
        lo, hi = _unpack_pairs(xbuf[slot].reshape(R, H))
        xbb[:, :H] = lo.astype(BF16)
        xbb[:, H:] = hi.astype(BF16)

        src_copy(srcn_ref).wait()
        dst_copy().wait()

        @pl.when(c + 1 < n_used)
        def _():
            issue_gather(1 - slot)

        xb = xbb[...]
        g = jnp.dot(xb, wgb[...], preferred_element_type=F32)
        u = jnp.dot(xb, wub[...], preferred_element_type=F32)
        y = jnp.dot((_silu(g) * u).astype(BF16), wdb[...], preferred_element_type=F32)
        ybuf[slot] = _pack_pairs(y).reshape(R // SUBLANES, SUBLANES, H)
        for_rows(lambda i, u_: scatter_copy(dst_s[0, 0, i * SUBLANES + u_], slot, i, u_).start())

    @pl.when(c == n_steps - 1)
    def _():
        last = n_used - 1
        whole_buffer_wait(ybuf, ssem, last % 2)

        @pl.when(n_used >= 2)
        def _():
            whole_buffer_wait(ybuf, ssem, (last - 1) % 2)

        zbuf[...] = jnp.zeros(zbuf.shape, U32)
        dump0 = y_hbm.shape[0] - 2 * R
        for s in range(2):
            cp = pltpu.make_async_copy(zbuf, y_hbm.at[pl.ds(dump0 + s * R, R), :], ssem.at[0])
            cp.start()
            cp.wait()


def _moe(h2p, idx, w_gate_e, w_up_e, w_down_e):
    M, _, H = h2p.shape
    E, D, De = w_gate_e.shape
    assert D == 2 * H
    R = MOE_ROWS
    A = M * TOP_K
    nch = A // R + E
    flat_e = idx.reshape(A)
    order = jnp.argsort(flat_e, stable=True).astype(I32)
    experts = jnp.arange(E, dtype=I32)
    counts = jnp.sum((flat_e[None, :] == experts[:, None]).astype(I32), axis=1)
    start = jnp.cumsum(counts).astype(I32) - counts
    nch_e = (counts + R - 1) // R
    ch_end = jnp.cumsum(nch_e).astype(I32)
    ch_start = ch_end - nch_e
    n_used = ch_end[-1]
    cid = jnp.arange(nch, dtype=I32)
    ce = jnp.sum((ch_end[None, :] <= jnp.minimum(cid, n_used - 1)[:, None]).astype(I32), axis=1)
    ce = jnp.minimum(ce, E - 1)
    after = ch_end[ce]
    nxt = jnp.where(after < n_used, ce[jnp.minimum(after, nch - 1)], -1).astype(I32)
    rr = jnp.arange(R, dtype=I32)[None, :]
    local = (cid - ch_start[ce])[:, None] * R + rr
    valid = (cid[:, None] < n_used) & (local < counts[ce][:, None])
    a = order[jnp.clip(start[ce][:, None] + local, 0, A - 1)]
    src = jnp.where(valid, a // TOP_K, 0)
    dst = jnp.where(valid, (a % TOP_K) * M + a // TOP_K, TOP_K * M + (cid[:, None] % 2) * R + rr)
    meta = jnp.stack([n_used, n_used]).astype(I32)

    idx_spec = lambda f: pl.BlockSpec((1, 1, R), f)
    hbm = pl.BlockSpec(memory_space=pl.ANY)
    return pl.pallas_call(
        _moe_kernel,
        grid_spec=pltpu.PrefetchScalarGridSpec(
            num_scalar_prefetch=3,
            grid=(nch,),
            in_specs=[idx_spec(lambda c, *_: (jnp.minimum(c + 1, nch - 1), 0, 0)),
                      idx_spec(lambda c, *_: (c, 0, 0)),
                      idx_spec(lambda c, *_: (c, 0, 0)),
                      hbm, hbm, hbm, hbm],
            out_specs=hbm,
            scratch_shapes=[pltpu.VMEM((2, R // SUBLANES, SUBLANES, H), U32),
                            pltpu.VMEM((2, R // SUBLANES, SUBLANES, H), U32), pltpu.VMEM((R, H), U32),
                            pltpu.VMEM((R, D), BF16), pltpu.VMEM((D, De), F32), pltpu.VMEM((D, De), F32), pltpu.VMEM((De, D), F32),
                            pltpu.VMEM((D, De), BF16), pltpu.VMEM((D, De), BF16),
                            pltpu.VMEM((De, D), BF16),
                            pltpu.SMEM((1, 1, R), I32), pltpu.SMEM((1, 1, R), I32),
                            pltpu.SemaphoreType.DMA((2,)), pltpu.SemaphoreType.DMA((2,)),
                            pltpu.SemaphoreType.DMA((2,)), pltpu.SemaphoreType.DMA((3,))]),
        out_shape=jax.ShapeDtypeStruct((TOP_K * M + 2 * R, H), U32),
        compiler_params=_cparams(("arbitrary",)),
        name="moe",
    )(ce, nxt, meta, src.reshape(nch, 1, R), src.reshape(nch, 1, R), dst.reshape(nch, 1, R),
      h2p, w_gate_e, w_up_e, w_down_e)


def _combine_kernel(base_ref, g2_ref, gw_ref, *refs):
    o_ref = refs[-1]
    H = o_ref.shape[-1] // 2
    gates = [gw_ref[:, k:k + 1] for k in range(TOP_K)]
    for j in range(0, H, LANES):
        halves = [_unpack_pairs(r[:, j:j + LANES]) for r in refs[:-1]]
        for part, sl in ((0, slice(j, j + LANES)), (1, slice(H + j, H + j + LANES))):
            routed = _tree_sum([h[part] * w for h, w in zip(halves, gates)])
            o_ref[:, sl] = base_ref[:, sl] + g2_ref[:, sl] * routed


def _combine(base, g2, y, gw, *, m_total, tok0, tm, per_token):
    Bx, T, D = base.shape
    nt = T // tm

    def plane(kk):
        off = (kk * m_total + tok0) // tm
        return pl.BlockSpec((tm, D // 2), lambda b, t: (off + b * nt + t, 0))

    return pl.pallas_call(
        _combine_kernel,
        grid=(Bx, nt),
        in_specs=[pl.BlockSpec((None, tm, D), lambda b, t: (b, t, 0)), _mod_spec(per_token, tm, D),
                  pl.BlockSpec((tm, LANES), lambda b, t: (tok0 // tm + b * nt + t, 0))]
                 + [plane(kk) for kk in range(TOP_K)],
        out_specs=pl.BlockSpec((None, tm, D), lambda b, t: (b, t, 0)),
        out_shape=jax.ShapeDtypeStruct((Bx, T, D), F32),
        compiler_params=_cparams(("parallel", "arbitrary")),
        name="combine",
    )(base, g2, gw, *([y] * TOP_K))


def _repack_w_in(w_in, b_in):
    c_ki = 3 * D_ATTN + IDX_HEADS * IDX_DIM
    c_wi = c_ki + IDX_DIM
    c_uv = c_wi + IDX_HEADS

    def pack(a):
        z = lambda n: jnp.zeros(a.shape[:-1] + (n,), a.dtype)
        return jnp.concatenate([a[..., :c_wi], z(C_WI - C_KI - IDX_DIM), a[..., c_wi:c_uv],
                                z(C_UV - C_WI - IDX_HEADS), a[..., c_uv:]], axis=-1)

    return pack(w_in).astype(BF16), pack(b_in[None, :])


def _pad_rows(a, n):
    return jnp.pad(a, ((0, 0), (0, n - a.shape[1]), (0, 0)))


def kernel(x_prompt, x_sample, cache_k, cache_v, cache_kidx, state_conv, c_prompt, c_sample, norm1_g, norm2_g, w_ada, b_ada, w_in, b_in, q_norm_g, k_norm_g, idx_k_ln_g, idx_k_ln_b, dw_w, dw_b, conv_ln_g, conv_ln_b, w_out, b_out, w_router, router_bias, w_gate_e, w_up_e, w_down_e, w_sh_gate, w_sh_up, w_sh_down):
    B, T, D = x_prompt.shape
    Bd, Td, _ = x_sample.shape
    depth = w_in.shape[0]
    past = cache_k.shape[2]
    Ms, Mp = Bd * Td, B * T
    M = Mp + Ms
    C = D - D_ATTN
    topk_p = min(IDX_TOPK_MAX, T // 4)
    topk_s = min(IDX_TOPK_MAX, (past + Td) // 4)
    l_s = past + Td
    s_pad = -(-l_s // KC) * KC
    tm_p, tm_s = 256, 128
    xp, xs = x_prompt, x_sample.reshape(1, Ms, D)
    outs = [[] for _ in range(8)]
    for l in range(depth):
        mods = _ada(jnp.concatenate([c_prompt, c_sample], axis=0), w_ada[l], b_ada[l])
        mp = [m[:, None, :] for m in jnp.split(mods[:B], 6, axis=-1)]
        ms = [jnp.repeat(m, Td, axis=0)[None] for m in jnp.split(mods[B:], 6, axis=-1)]
        w2, b2 = _repack_w_in(w_in[l], b_in[l])
        row = lambda a: a.reshape(1, -1)
        proj_w = (row(norm1_g[l]), w2, b2, row(q_norm_g[l]), row(k_norm_g[l]),
                  row(idx_k_ln_g[l]), row(idx_k_ln_b[l]))
        woa, wob = w_out[l, :D_ATTN].astype(BF16), w_out[l, D_ATTN:].astype(BF16)
        out_w = (row(norm2_g[l]), woa, wob, row(b_out[l]), w_router[l].astype(BF16),
                 w_sh_gate[l].astype(BF16), w_sh_up[l].astype(BF16), w_sh_down[l].astype(BF16))

        q, k, kb, v, vb, qi, ki, kib, wi, glu = _inproj(xp, mp[0], mp[1], *proj_w, tm=tm_p, per_token=False)
        attn = _dsa(q, qi, wi, kb, vb, kib, q_base=0, l_valid=T, topk=topk_p, causal=True)
        conv = _conv(glu, jnp.zeros((B, CTX_PAD, C), F32), dw_w[l], dw_b[l], conv_ln_g[l], conv_ln_b[l])
        base_p, h2_p, lg_p = _outproj(attn, conv, xp, mp[2], mp[3], mp[4], mp[5], *out_w,
                                      tm=tm_p, per_token=False)
        outs[0].append(k.reshape(B, T, N_HEADS, HEAD_DIM))
        outs[1].append(v.reshape(B, T, N_HEADS, HEAD_DIM))
        outs[2].append(ki)
        outs[3].append(glu[:, T - CTX:])

        q, k, kb, v, vb, qi, ki, kib, wi, glu = _inproj(xs, ms[0], ms[1], *proj_w, tm=tm_s, per_token=True)
        per_b = lambda a: a.reshape(Bd, Td, a.shape[-1])
        cat = lambda cache, new: _pad_rows(
            jnp.concatenate([cache.reshape(Bd, past, -1).astype(BF16), per_b(new)], axis=1), s_pad)
        attn = _dsa(_pad_rows(per_b(q), QB), _pad_rows(per_b(qi), QB), _pad_rows(per_b(wi), QB),
                    cat(cache_k[l], kb), cat(cache_v[l], vb), cat(cache_kidx[l], kib),
                    q_base=past, l_valid=l_s, topk=topk_s, causal=False)
        attn = attn[:, :Td].reshape(1, Ms, D_ATTN)
        ext = jnp.concatenate([state_conv[l], per_b(glu)], axis=1)
        ctx = jnp.pad(state_conv[l], ((0, 0), (CTX_PAD - CTX, 0), (0, 0)))
        conv = _conv(per_b(glu), ctx, dw_w[l], dw_b[l], conv_ln_g[l], conv_ln_b[l]).reshape(1, Ms, C)
        base_s, h2_s, lg_s = _outproj(attn, conv, xs, ms[2], ms[3], ms[4], ms[5], *out_w,
                                      tm=tm_s, per_token=True)
        outs[4].append(per_b(k).reshape(Bd, Td, N_HEADS, HEAD_DIM))
        outs[5].append(per_b(v).reshape(Bd, Td, N_HEADS, HEAD_DIM))
        outs[6].append(per_b(ki))
        outs[7].append(ext[:, Td:])

        h2 = _concat_rows(h2_p.reshape(Mp, 1, D // 2), h2_s.reshape(Ms, 1, D // 2))
        lg = jnp.concatenate([lg_p.reshape(Mp, -1), lg_s.reshape(Ms, -1)], axis=0)
        idx, gw = _route(lg, router_bias[l])
        y = _moe(h2, idx[:, :TOP_K], w_gate_e[l], w_up_e[l], w_down_e[l])
        xp = _combine(base_p, mp[5], y, gw, m_total=M, tok0=0, tm=tm_p, per_token=False)
        xs = _combine(base_s, ms[5], y, gw, m_total=M, tok0=Mp, tm=256, per_token=True)
    return (xp, xs.reshape(Bd, Td, D), *[jnp.stack(o) for o in outs])
```
